```python
import math
import jax, jax.numpy as jnp
from jax import lax
import numpy as np

D_MODEL = 4096
BATCH = 4
SEQ = 4096
DEPTH = 4

N_MIXERS = 4
N_META = 16
BRANCH = D_MODEL
ALPHA = (2.0 * DEPTH) ** 0.25
BETA = (8.0 * DEPTH) ** -0.25
LN_EPS = 1e-5
RMS_EPS = 1e-6
ROPE_BASE = 10000.0

LRU_HEADS = 16
LRU_BLOCK = BRANCH // LRU_HEADS
CONV_W = 4
LRU_C = 8.0

POOL_WINDOWS = (2, 4, 8, 16)
POOL_GROUP = BRANCH // len(POOL_WINDOWS)

MLA_HEADS = 32
MLA_NOPE = 128
MLA_ROPE = 64
MLA_V = BRANCH // MLA_HEADS
Q_LORA = 1024
KV_LORA = 512
ATT_BLOCK = 128

RET_HEADS = 16
RET_DK = BRANCH // RET_HEADS
RET_DV = BRANCH // RET_HEADS
RET_CHUNK = 128

kernel_name = 'hybrid_interleaved_rglru_pool_mla_retention'


def layer_norm(x, g, b):
    xf = x.astype(jnp.float32)
    mu = jnp.mean(xf, -1, keepdims=True)
    var = jnp.mean(jnp.square(xf - mu), -1, keepdims=True)
    return ((xf - mu) * lax.rsqrt(var + LN_EPS) * g.astype(jnp.float32) + b.astype(jnp.float32)).astype(x.dtype)


def rms_norm(x, g):
    xf = x.astype(jnp.float32)
    return (xf * lax.rsqrt(jnp.mean(jnp.square(xf), -1, keepdims=True) + RMS_EPS) * g.astype(jnp.float32)).astype(x.dtype)


def rope_tables(T, d):
    inv = ROPE_BASE ** (-jnp.arange(0, d, 2, dtype=jnp.float32) / d)
    ang = jnp.arange(T, dtype=jnp.float32)[:, None] * inv[None, :]
    return jnp.cos(ang), jnp.sin(ang)


def apply_rope(x, cos, sin):
    x1, x2 = jnp.split(x.astype(jnp.float32), 2, axis=-1)
    return jnp.concatenate([x1 * cos - x2 * sin, x2 * cos + x1 * sin], -1).astype(x.dtype)


def rglru_mixer(h, w_in, conv_w, conv_b, w_a, b_a, w_x, b_x, lam, w_out):
    B, T, _ = h.shape
    u, g = jnp.split(h @ w_in, 2, axis=-1)
    u = lax.conv_general_dilated(u, conv_w, window_strides=(1,), padding=[(CONV_W - 1, 0)],
                                 dimension_numbers=('NWC', 'WIO', 'NWC'),
                                 feature_group_count=BRANCH) + conv_b
    ub = u.reshape(B, T, LRU_HEADS, LRU_BLOCK)
    r = jax.nn.sigmoid(jnp.einsum('bthi,hij->bthj', ub, w_a).reshape(B, T, BRANCH) + b_a)
    i = jax.nn.sigmoid(jnp.einsum('bthi,hij->bthj', ub, w_x).reshape(B, T, BRANCH) + b_x)
    log_a = LRU_C * r.astype(jnp.float32) * (-jax.nn.softplus(-lam.astype(jnp.float32)))
    a = jnp.exp(log_a)
    xin = (u * i).astype(jnp.float32) * jnp.sqrt(-jnp.expm1(2.0 * log_a))

    def combine(left, right):
        a1, b1 = left
        a2, b2 = right
        return a1 * a2, a2 * b1 + b2

    _, hs = lax.associative_scan(combine, (a, xin), axis=1)
    return (hs.astype(h.dtype) * jax.nn.silu(g)) @ w_out


def pool_mixer(h, w_in, w_grp, scale, w_out):
    B, T, _ = h.shape
    u, g = jnp.split(h @ w_in, 2, axis=-1)
    uf = u.astype(jnp.float32)
    c = jnp.cumsum(uf, axis=1)
    t1 = jnp.arange(1, T + 1, dtype=jnp.float32)[:, None]
    outs = []
    for gi, w in enumerate(POOL_WINDOWS):
        sl = slice(gi * POOL_GROUP, (gi + 1) * POOL_GROUP)
        cg = c[..., sl]
        shifted = jnp.pad(cg, ((0, 0), (w, 0), (0, 0)))[:, :T]
        mean = (cg - shifted) / jnp.minimum(t1, float(w))
        outs.append(mean - uf[..., sl])
    p = jnp.stack(outs, axis=2).astype(h.dtype)
    mixed = jnp.einsum('btgc,gcd->btgd', p, w_grp).reshape(B, T, BRANCH) * scale
    return (mixed * jax.nn.silu(g)) @ w_out


def mla_mixer(h, w_in, q_norm, w_uq, kv_norm, w_ukv, w_out, cos, sin):
    B, T, _ = h.shape
    g, cq, ckv, kr = jnp.split(h @ w_in, [BRANCH, BRANCH + Q_LORA, BRANCH + Q_LORA + KV_LORA], axis=-1)
    q = (rms_norm(cq, q_norm) @ w_uq).reshape(B, T, MLA_HEADS, MLA_NOPE + MLA_ROPE)
    q_nope = q[..., :MLA_NOPE]
    q_rope = apply_rope(q[..., MLA_NOPE:], cos[:, None, :], sin[:, None, :])
    kv = (rms_norm(ckv, kv_norm) @ w_ukv).reshape(B, T, MLA_HEADS, MLA_NOPE + MLA_V)
    k_nope, v = kv[..., :MLA_NOPE], kv[..., MLA_NOPE:]
    k_rope = apply_rope(kr, cos, sin)
    sc = (MLA_NOPE + MLA_ROPE) ** -0.5
    n_blk = (T - N_META) // ATT_BLOCK
    blocks = [(0, N_META)] + [(N_META + j * ATT_BLOCK, N_META + (j + 1) * ATT_BLOCK) for j in range(n_blk)]
    outs = []
    for qs, qe in blocks:
        s = (jnp.einsum('bqhd,bkhd->bhqk', q_nope[:, qs:qe], k_nope[:, :qe], preferred_element_type=jnp.float32)
             + jnp.einsum('bqhd,bkd->bhqk', q_rope[:, qs:qe], k_rope[:, :qe], preferred_element_type=jnp.float32)) * sc
        mask = jnp.arange(qs, qe)[:, None] >= jnp.arange(qe)[None, :]
        s = jnp.where(mask, s, -1e30)
        pr = jax.nn.softmax(s, axis=-1).astype(v.dtype)
        outs.append(jnp.einsum('bhqk,bkhd->bqhd', pr, v[:, :qe]))
    o = jnp.concatenate(outs, axis=1).reshape(B, T, BRANCH)
    return (o * jax.nn.silu(g)) @ w_out


def retention_mixer(h, w_in, w_out, cos, sin):
    B, T, _ = h.shape
    q, k, v, g = jnp.split(h @ w_in, 4, axis=-1)
    q = apply_rope(q.reshape(B, T, RET_HEADS, RET_DK), cos[:, None, :], sin[:, None, :])
    k = apply_rope(k.reshape(B, T, RET_HEADS, RET_DK), cos[:, None, :], sin[:, None, :]) * (RET_DK ** -0.5)
    v = v.reshape(B, T, RET_HEADS, RET_DV)
    log_g = jnp.log(1.0 - 2.0 ** (-5.0 - jnp.arange(RET_HEADS, dtype=jnp.float32)))
    qf, kf, vf = [t.astype(jnp.float32).transpose(0, 2, 1, 3) for t in (q, k, v)]

    def chunk_step(state, qkv):
        qc, kc, vc = qkv
        C = qc.shape[2]
        j = jnp.arange(C, dtype=jnp.float32)
        diff = j[:, None] - j[None, :]
        decay = jnp.where(diff >= 0, jnp.exp(jnp.maximum(diff, 0.0)[None] * log_g[:, None, None]), 0.0)
        intra = jnp.einsum('bhcm,bhme->bhce', jnp.einsum('bhcd,bhmd->bhcm', qc, kc) * decay, vc)
        inter = jnp.einsum('bhcd,bhde->bhce', qc, state) * jnp.exp((j + 1.0)[None, :, None] * log_g[:, None, None])
        kw = kc * jnp.exp((C - 1.0 - j)[None, :, None] * log_g[:, None, None])
        new_state = jnp.exp(C * log_g)[None, :, None, None] * state + jnp.einsum('bhcd,bhce->bhde', kw, vc)
        return new_state, intra + inter

    state0 = jnp.zeros((B, RET_HEADS, RET_DK, RET_DV), jnp.float32)
    state, o_meta = chunk_step(state0, (qf[:, :, :N_META], kf[:, :, :N_META], vf[:, :, :N_META]))
    nc = (T - N_META) // RET_CHUNK

    def to_chunks(t):
        return t[:, :, N_META:].reshape(B, RET_HEADS, nc, RET_CHUNK, t.shape[-1]).transpose(2, 0, 1, 3, 4)

    _, o_real = lax.scan(chunk_step, state, (to_chunks(qf), to_chunks(kf), to_chunks(vf)))
    o_real = o_real.transpose(1, 2, 0, 3, 4).reshape(B, RET_HEADS, nc * RET_CHUNK, RET_DV)
    o = jnp.concatenate([o_meta, o_real], axis=2)
    mu = jnp.mean(o, -1, keepdims=True)
    var = jnp.mean(jnp.square(o - mu), -1, keepdims=True)
    o = (o - mu) * lax.rsqrt(var + LN_EPS)
    o = o.transpose(0, 2, 1, 3).reshape(B, T, BRANCH).astype(h.dtype)
    return (o * jax.nn.silu(g)) @ w_out


def setup_inputs(seed: int = 0) -> dict:
    key = jax.random.key(seed)
    ks = iter(jax.random.split(key, 48))

    def nrm(shape, s):
        return jax.random.normal(next(ks), shape, jnp.float32) * s

    def gain(n):
        return 1.0 + nrm((n,), 0.02)

    d_in = D_MODEL ** -0.5
    d_out = BETA * BRANCH ** -0.5
    rad = jax.random.uniform(next(ks), (BRANCH,), jnp.float32, 0.9, 0.999)
    a0 = rad ** (1.0 / LRU_C)
    lam = jnp.log(a0) - jnp.log1p(-a0)
    return {
        'x': nrm((BATCH, SEQ, D_MODEL), 1.0),
        'meta_tokens': nrm((N_META, D_MODEL), 1.0),
        'l0_w_in': nrm((D_MODEL, 2 * BRANCH), d_in),
        'l0_conv_w': nrm((CONV_W, 1, BRANCH), CONV_W ** -0.5),
        'l0_conv_b': nrm((BRANCH,), 0.01),
        'l0_w_a': nrm((LRU_HEADS, LRU_BLOCK, LRU_BLOCK), LRU_BLOCK ** -0.5),
        'l0_b_a': nrm((BRANCH,), 0.01),
        'l0_w_x': nrm((LRU_HEADS, LRU_BLOCK, LRU_BLOCK), LRU_BLOCK ** -0.5),
        'l0_b_x': nrm((BRANCH,), 0.01),
        'l0_lam': lam,
        'l0_w_out': nrm((BRANCH, D_MODEL), d_out),
        'l0_ln_g': gain(D_MODEL),
        'l0_ln_b': nrm((D_MODEL,), 0.01),
        'l1_w_in': nrm((D_MODEL, 2 * BRANCH), d_in),
        'l1_w_grp': nrm((len(POOL_WINDOWS), POOL_GROUP, POOL_GROUP), POOL_GROUP ** -0.5),
        'l1_scale': gain(BRANCH),
        'l1_w_out': nrm((BRANCH, D_MODEL), d_out),
        'l1_ln_g': gain(D_MODEL),
        'l1_ln_b': nrm((D_MODEL,), 0.01),
        'l2_w_in': nrm((D_MODEL, BRANCH + Q_LORA + KV_LORA + MLA_ROPE), d_in),
        'l2_q_norm': gain(Q_LORA),
        'l2_w_uq': nrm((Q_LORA, MLA_HEADS * (MLA_NOPE + MLA_ROPE)), Q_LORA ** -0.5),
        'l2_kv_norm': gain(KV_LORA),
        'l2_w_ukv': nrm((KV_LORA, MLA_HEADS * (MLA_NOPE + MLA_V)), KV_LORA ** -0.5),
        'l2_w_out': nrm((BRANCH, D_MODEL), d_out),
        'l2_ln_g': gain(D_MODEL),
        'l2_ln_b': nrm((D_MODEL,), 0.01),
        'l3_w_in': nrm((D_MODEL, 4 * BRANCH), d_in),
        'l3_w_out': nrm((BRANCH, D_MODEL), d_out),
        'l3_ln_g': gain(D_MODEL),
        'l3_ln_b': nrm((D_MODEL,), 0.01),
    }


def reference(x, meta_tokens,
              l0_w_in, l0_conv_w, l0_conv_b, l0_w_a, l0_b_a, l0_w_x, l0_b_x, l0_lam, l0_w_out, l0_ln_g, l0_ln_b,
              l1_w_in, l1_w_grp, l1_scale, l1_w_out, l1_ln_g, l1_ln_b,
              l2_w_in, l2_q_norm, l2_w_uq, l2_kv_norm, l2_w_ukv, l2_w_out, l2_ln_g, l2_ln_b,
              l3_w_in, l3_w_out, l3_ln_g, l3_ln_b):
    B = x.shape[0]
    meta = jnp.broadcast_to(meta_tokens.astype(x.dtype)[None], (B, N_META, D_MODEL))
    h = jnp.concatenate([meta, x], axis=1)
    T = h.shape[1]
    cos_mla, sin_mla = rope_tables(T, MLA_ROPE)
    cos_ret, sin_ret = rope_tables(T, RET_DK)
    layer_fns = (
        lambda t: rglru_mixer(t, l0_w_in, l0_conv_w, l0_conv_b, l0_w_a, l0_b_a, l0_w_x, l0_b_x, l0_lam, l0_w_out),
        lambda t: pool_mixer(t, l1_w_in, l1_w_grp, l1_scale, l1_w_out),
        lambda t: mla_mixer(t, l2_w_in, l2_q_norm, l2_w_uq, l2_kv_norm, l2_w_ukv, l2_w_out, cos_mla, sin_mla),
        lambda t: retention_mixer(t, l3_w_in, l3_w_out, cos_ret, sin_ret),
    )
    norms = ((l0_ln_g, l0_ln_b), (l1_ln_g, l1_ln_b), (l2_ln_g, l2_ln_b), (l3_ln_g, l3_ln_b))
    for i in range(DEPTH):
        h = layer_norm(ALPHA * h + layer_fns[i](h), norms[i][0], norms[i][1])
    return h[:, N_META:]
```

```python
import functools

import jax
import jax.numpy as jnp
from jax import lax
from jax.experimental import pallas as pl
from jax.experimental.pallas import tpu as pltpu

F32 = jnp.float32
BF16 = jnp.bfloat16

DEPTH = 4
ALPHA = (2.0 * DEPTH) ** 0.25
LN_EPS = 1e-5
RMS_EPS = 1e-6
ROPE_BASE = 10000.0
LRU_C = 8.0
POOL_WINDOWS = (2, 4, 8, 16)
MLA_NOPE = 128
MLA_ROPE = 64

V7X_LANES = 128
V7X_SUBLANES = 8
V7X_BF16_ROWS = 16
V7X_SCOPED_VMEM_MAX = 60000 * 1024
V7X_VMEM_INTERNAL = 12 * 1024 * 1024

TIME_ALIGN = 128


def _pick_tile(n, target, mult):
    best = None
    for d in range(mult, min(n, target) + 1, mult):
        if n % d == 0:
            best = d
    return n if best is None else best


def _params(semantics, block_bytes):
    limit = min(2 * block_bytes + V7X_VMEM_INTERNAL, V7X_SCOPED_VMEM_MAX)
    return pltpu.CompilerParams(dimension_semantics=semantics, vmem_limit_bytes=int(limit))


def _nbytes(shape, dtype):
    n = 1
    for s in shape:
        n *= s
    return n * jnp.dtype(dtype).itemsize


def _sigmoid(x):
    return 1.0 / (1.0 + jnp.exp(-x))


def _silu(x):
    return x * _sigmoid(x)


def _mm_body(a_ref, w_ref, o_ref):
    o_ref[...] = jnp.dot(a_ref[...], w_ref[...], preferred_element_type=F32).astype(o_ref.dtype)


def _matmul(a, w, out_dtype, name, tm_target=1056, tn_target=1024):
    M, K = a.shape
    N = w.shape[1]
    tm = _pick_tile(M, tm_target, V7X_BF16_ROWS)
    tn = _pick_tile(N, tn_target, V7X_LANES)
    blk = _nbytes((tm, K), a.dtype) + _nbytes((K, tn), w.dtype) + _nbytes((tm, tn), out_dtype)
    return pl.pallas_call(
        _mm_body,
        grid=(M // tm, N // tn),
        in_specs=[pl.BlockSpec((tm, K), lambda i, j: (i, 0)),
                  pl.BlockSpec((K, tn), lambda i, j: (0, j))],
        out_specs=pl.BlockSpec((tm, tn), lambda i, j: (i, j)),
        out_shape=jax.ShapeDtypeStruct((M, N), out_dtype),
        compiler_params=_params(("parallel", "arbitrary"), blk),
        name=name,
    )(a, w)


LN_ROWS = 16


def _mm_res_ln_body(y_ref, w_ref, h_ref, g_ref, b_ref, oh_ref, ohb_ref, *, nj, tn):
    j = pl.program_id(1)
    z = ALPHA * h_ref[...] + jnp.dot(y_ref[...], w_ref[...], preferred_element_type=F32)
    for jj in range(nj):
        @pl.when(j == jj)
        def _(jj=jj):
            oh_ref[:, jj * tn:(jj + 1) * tn] = z

    @pl.when(j == nj - 1)
    def _():
        tm = oh_ref.shape[0]

        def chunk(r, carry):
            rows = pl.ds(pl.multiple_of(r * LN_ROWS, LN_ROWS), LN_ROWS)
            zr = oh_ref[rows, :]
            mu = jnp.mean(zr, axis=1, keepdims=True)
            var = jnp.mean(jnp.square(zr - mu), axis=1, keepdims=True)
            o = (zr - mu) * lax.rsqrt(var + LN_EPS) * g_ref[...] + b_ref[...]
            oh_ref[rows, :] = o
            ohb_ref[rows, :] = o.astype(BF16)
            return carry

        lax.fori_loop(0, tm // LN_ROWS, chunk, 0)


def _mm_res_ln(y, w, h, gamma, beta, name, tm_target=528, tn_target=512):
    M, K = y.shape
    D = w.shape[1]
    tm = _pick_tile(M, tm_target, V7X_BF16_ROWS)
    tn = _pick_tile(D, tn_target, V7X_LANES)
    nj = D // tn
    blk = (_nbytes((tm, K), BF16) + _nbytes((K, tn), BF16) + _nbytes((tm, tn), F32)
           + _nbytes((tm, D), F32) + _nbytes((tm, D), BF16))
    body = functools.partial(_mm_res_ln_body, nj=nj, tn=tn)
    return pl.pallas_call(
        body,
        grid=(M // tm, nj),
        in_specs=[pl.BlockSpec((tm, K), lambda i, j: (i, 0)),
                  pl.BlockSpec((K, tn), lambda i, j: (0, j)),
                  pl.BlockSpec((tm, tn), lambda i, j: (i, j)),
                  pl.BlockSpec((1, D), lambda i, j: (0, 0)),
                  pl.BlockSpec((1, D), lambda i, j: (0, 0))],
        out_specs=[pl.BlockSpec((tm, D), lambda i, j: (i, 0)),
                   pl.BlockSpec((tm, D), lambda i, j: (i, 0))],
        out_shape=[jax.ShapeDtypeStruct((M, D), F32), jax.ShapeDtypeStruct((M, D), BF16)],
        compiler_params=_params(("parallel", "arbitrary"), blk),
        name=name,
    )(y, w, h, gamma.reshape(1, D), beta.reshape(1, D))


def _shift_rows(v, d, fill, row):
    n = v.shape[0]
    if d % V7X_SUBLANES == 0:
        return jnp.concatenate([jnp.full((d,) + v.shape[1:], fill, v.dtype), v[:n - d]], axis=0)
    return jnp.where(row >= d, pltpu.roll(v, d, 0), fill)


CONV_HIST = 8
SCAN_ROWS = 32


def _rglru_body(u_ref, g_ref, cw_ref, cb_ref, wg_ref, bg_ref, lam_ref, y_ref,
                ubuf, hcar, sa, sx, *, conv_w, hpb, blk):
    t = pl.program_id(2)
    tt = u_ref.shape[1]
    cw = u_ref.shape[2]

    @pl.when(t == 0)
    def _():
        ubuf[0:CONV_HIST, :] = jnp.zeros((CONV_HIST, cw), F32)
        hcar[...] = jnp.zeros_like(hcar)

    ubuf[CONV_HIST:CONV_HIST + tt, :] = u_ref[0]
    uc = cb_ref[...] + cw_ref[conv_w - 1:conv_w, :] * ubuf[CONV_HIST:CONV_HIST + tt, :]
    for k in range(conv_w - 1):
        off = CONV_HIST - (conv_w - 1) + k
        uc = uc + cw_ref[k:k + 1, :] * ubuf[off:off + tt, :]
    ubuf[0:CONV_HIST, :] = ubuf[tt:tt + CONV_HIST, :]

    lam = lam_ref[...]
    logsig = -(jnp.maximum(-lam, 0.0) + jnp.log1p(jnp.exp(-jnp.abs(lam))))
    for hh in range(hpb):
        cols = slice(hh * blk, (hh + 1) * blk)
        ucs = uc[:, cols]
        gate = jnp.dot(ucs.astype(BF16), wg_ref[hh], preferred_element_type=F32) + bg_ref[hh]
        r = _sigmoid(gate[:, :blk])
        i = _sigmoid(gate[:, blk:])
        a = jnp.exp(LRU_C * r * logsig[:, cols])
        sa[:, cols] = a
        sx[:, cols] = (ucs * i) * jnp.sqrt(1.0 - a * a)

    row = lax.broadcasted_iota(jnp.int32, (SCAN_ROWS, cw), 0)
    h_in = hcar[...]
    for s in range(tt // SCAN_ROWS):
        rows = slice(s * SCAN_ROWS, (s + 1) * SCAN_ROWS)
        a = sa[rows, :]
        x = sx[rows, :]
        d = 1
        while d < SCAN_ROWS:
            x = a * _shift_rows(x, d, 0.0, row) + x
            a = a * _shift_rows(a, d, 1.0, row)
            d *= 2
        h = x + a * h_in
        h_in = h[SCAN_ROWS - 1:SCAN_ROWS, :]
        y_ref[0, rows, :] = (h * _silu(g_ref[0, rows, :])).astype(BF16)
    hcar[...] = h_in


def _rglru_mix(ug, conv_w, conv_b, w_a, b_a, w_x, b_x, lam, B, T_pad):
    BR = conv_b.shape[0]
    heads, blk, _ = w_a.shape
    K = conv_w.shape[0]
    hpb = 2 if heads % 2 == 0 else 1
    cw = hpb * blk
    ncb = BR // cw
    tt = _pick_tile(T_pad, 384, SCAN_ROWS)
    wg = jnp.concatenate([w_a, w_x], axis=-1).astype(BF16)
    bg = jnp.concatenate([b_a.reshape(heads, 1, blk), b_x.reshape(heads, 1, blk)], axis=-1)
    ug3 = ug.reshape(B, T_pad, 2 * BR)
    body = functools.partial(_rglru_body, conv_w=K, hpb=hpb, blk=blk)
    blkbytes = 3 * _nbytes((tt, cw), F32) + _nbytes((hpb, blk, 2 * blk), BF16) + 2 * _nbytes((tt, cw), F32)
    return pl.pallas_call(
        body,
        grid=(B, ncb, T_pad // tt),
        in_specs=[pl.BlockSpec((1, tt, cw), lambda b, c, t: (b, t, c)),
                  pl.BlockSpec((1, tt, cw), lambda b, c, t: (b, t, c + ncb)),
                  pl.BlockSpec((K, cw), lambda b, c, t: (0, c)),
                  pl.BlockSpec((1, cw), lambda b, c, t: (0, c)),
                  pl.BlockSpec((hpb, blk, 2 * blk), lambda b, c, t: (c, 0, 0)),
                  pl.BlockSpec((hpb, 1, 2 * blk), lambda b, c, t: (c, 0, 0)),
                  pl.BlockSpec((1, cw), lambda b, c, t: (0, c))],
        out_specs=pl.BlockSpec((1, tt, cw), lambda b, c, t: (b, t, c)),
        out_shape=jax.ShapeDtypeStruct((B, T_pad, BR), BF16),
        scratch_shapes=[pltpu.VMEM((CONV_HIST + tt, cw), F32),
                        pltpu.VMEM((1, cw), F32),
                        pltpu.VMEM((tt, cw), F32),
                        pltpu.VMEM((tt, cw), F32)],
        compiler_params=_params(("parallel", "parallel", "arbitrary"), blkbytes),
        name="rglru_mix",
    )(ug3, ug3, conv_w.reshape(K, BR), conv_b.reshape(1, BR), wg, bg, lam.reshape(1, BR))


POOL_HIST = 24


def _pool_body(u_ref, g_ref, w_ref, sc_ref, y_ref, ubuf, sbuf, pbuf, *, windows):
    grp = pl.program_id(0)
    t = pl.program_id(2)
    tt = u_ref.shape[1]
    gw = u_ref.shape[2]
    n = POOL_HIST + tt

    @pl.when(t == 0)
    def _():
        ubuf[0:POOL_HIST, :] = jnp.zeros((POOL_HIST, gw), F32)

    sbuf[0:V7X_SUBLANES, :] = jnp.zeros((V7X_SUBLANES, gw), F32)
    ubuf[POOL_HIST:n, :] = u_ref[0]
    pos = lax.broadcasted_iota(jnp.int32, (tt, 1), 0) + t * tt + 1

    for gi, w in enumerate(windows):
        @pl.when(grp == gi)
        def _(w=w):
            lo = V7X_SUBLANES
            s = ubuf[lo:n, :] + ubuf[lo - 1:n - 1, :]
            d = 2
            while d < w:
                sbuf[lo:n, :] = s
                s = sbuf[lo:n, :] + sbuf[lo - d:n - d, :]
                d *= 2
            wsum = s[POOL_HIST - lo:, :]
            cnt = jnp.minimum(pos, w).astype(F32)
            pbuf[...] = (wsum / cnt - u_ref[0]).astype(BF16)

    ubuf[0:POOL_HIST, :] = ubuf[tt:n, :]
    mixed = jnp.dot(pbuf[...], w_ref[0], preferred_element_type=F32) * sc_ref[...]
    y_ref[0] = (mixed * _silu(g_ref[0])).astype(BF16)


def _pool_mix(ug, w_grp, scale, B, T_pad):
    ngrp, gw, _ = w_grp.shape
    BR = ngrp * gw
    tt = _pick_tile(T_pad, 384, V7X_BF16_ROWS)
    ug3 = ug.reshape(B, T_pad, 2 * BR)
    body = functools.partial(_pool_body, windows=POOL_WINDOWS[:ngrp])
    blkbytes = 5 * _nbytes((tt, gw), F32) + _nbytes((gw, gw), BF16)
    return pl.pallas_call(
        body,
        grid=(ngrp, B, T_pad // tt),
        in_specs=[pl.BlockSpec((1, tt, gw), lambda g, b, t: (b, t, g)),
                  pl.BlockSpec((1, tt, gw), lambda g, b, t: (b, t, g + ngrp)),
                  pl.BlockSpec((1, gw, gw), lambda g, b, t: (g, 0, 0)),
                  pl.BlockSpec((1, gw), lambda g, b, t: (0, g))],
        out_specs=pl.BlockSpec((1, tt, gw), lambda g, b, t: (b, t, g)),
        out_shape=jax.ShapeDtypeStruct((B, T_pad, BR), BF16),
        scratch_shapes=[pltpu.VMEM((POOL_HIST + tt, gw), F32),
                        pltpu.VMEM((POOL_HIST + tt, gw), F32),
                        pltpu.VMEM((tt, gw), BF16)],
        compiler_params=_params(("parallel", "parallel", "arbitrary"), blkbytes),
        name="pool_mix",
    )(ug3, ug3, w_grp.astype(BF16), scale.reshape(1, BR))


def _rope_pairs(x, c_ref, s_ref):
    half = MLA_ROPE // 2
    lane = lax.broadcasted_iota(jnp.int32, x.shape, 1)
    partner = jnp.where(lane % MLA_ROPE < half,
                        pltpu.roll(x, V7X_LANES - half, 1), pltpu.roll(x, half, 1))
    return x * c_ref[...] + partner * s_ref[...]


def _rms_scale(x, g):
    return x * lax.rsqrt(jnp.mean(jnp.square(x), axis=-1, keepdims=True) + RMS_EPS) * g


def _qproj_body(c_ref, n_ref, w_ref, cos_ref, sin_ref, q_ref, xn_ref):
    @pl.when(pl.program_id(1) == 0)
    def _():
        xn_ref[...] = _rms_scale(c_ref[...], n_ref[...]).astype(BF16)

    res = jnp.dot(xn_ref[...], w_ref[...], preferred_element_type=F32)
    nn = 2 * MLA_NOPE
    q_ref[:, :nn] = res[:, :nn].astype(BF16)
    q_ref[:, nn:] = _rope_pairs(res[:, nn:], cos_ref, sin_ref).astype(BF16)


def _kvproj_body(c_ref, kr_ref, n_ref, w_ref, cos_ref, sin_ref, kv_ref, krd_ref, xn_ref):
    @pl.when(pl.program_id(1) == 0)
    def _():
        xn_ref[...] = _rms_scale(c_ref[...], n_ref[...]).astype(BF16)
        rot = _rope_pairs(kr_ref[...], cos_ref, sin_ref)
        lane = lax.broadcasted_iota(jnp.int32, rot.shape, 1)
        krd_ref[...] = jnp.where(lane < MLA_ROPE, rot, pltpu.roll(rot, MLA_ROPE, 1)).astype(BF16)

    kv_ref[...] = jnp.dot(xn_ref[...], w_ref[...], preferred_element_type=F32).astype(BF16)


def _attn_body(q_ref, kn_ref, v_ref, kr_ref, g_ref, y_ref, m_ref, l_ref, acc_ref, *, tk, scale):
    qi = pl.program_id(2)
    tq = q_ref.shape[1]
    nn = 2 * MLA_NOPE
    lane = lax.broadcasted_iota(jnp.int32, (tq, V7X_LANES), 1)
    qr = q_ref[0, :, nn:]
    zero = jnp.zeros_like(qr)
    qs = (jnp.concatenate([q_ref[0, :, :MLA_NOPE], jnp.where(lane < MLA_ROPE, qr, zero)], axis=1),
          jnp.concatenate([q_ref[0, :, MLA_NOPE:nn], jnp.where(lane >= MLA_ROPE, qr, zero)], axis=1))

    m_ref[...] = jnp.full_like(m_ref, -jnp.inf)
    l_ref[...] = jnp.zeros_like(l_ref)
    acc_ref[...] = jnp.zeros_like(acc_ref)

    def block(ki, masked):
        k0 = pl.multiple_of(ki * tk, tk)
        krope = kr_ref[0, pl.ds(k0, tk), :]
        for hh in range(2):
            hc = slice(hh * MLA_NOPE, (hh + 1) * MLA_NOPE)
            kf = jnp.concatenate([kn_ref[0, pl.ds(k0, tk), hc], krope], axis=1)
            s = lax.dot_general(qs[hh], kf, (((1,), (1,)), ((), ())),
                                preferred_element_type=F32) * scale
            if masked:
                rq = lax.broadcasted_iota(jnp.int32, (tq, tk), 0)
                ck = lax.broadcasted_iota(jnp.int32, (tq, tk), 1)
                s = jnp.where(rq >= ck, s, -1e30)
            m_old = m_ref[hh]
            m_new = jnp.maximum(m_old, s.max(axis=1, keepdims=True))
            p = jnp.exp(s - m_new)
            corr = jnp.exp(m_old - m_new)
            l_ref[hh] = corr * l_ref[hh] + p.sum(axis=1, keepdims=True)
            acc_ref[hh] = corr * acc_ref[hh] + jnp.dot(
                p.astype(BF16), v_ref[0, pl.ds(k0, tk), hc], preferred_element_type=F32)
            m_ref[hh] = m_new

    def loop_body(ki, carry):
        block(ki, False)
        return carry

    lax.fori_loop(0, qi, loop_body, 0)
    block(qi, True)

    gate = _silu(g_ref[0])
    for hh in range(2):
        hc = slice(hh * MLA_NOPE, (hh + 1) * MLA_NOPE)
        o = acc_ref[hh] / l_ref[hh]
        y_ref[0, :, hc] = (o * gate[:, hc]).astype(BF16)


def _mla_mix(hb, w_in, q_norm, w_uq, kv_norm, w_ukv, B, T_pad):
    q_lora = q_norm.shape[0]
    kv_lora = kv_norm.shape[0]
    heads = w_uq.shape[1] // (MLA_NOPE + MLA_ROPE)
    vdim = w_ukv.shape[1] // heads - MLA_NOPE
    assert vdim == MLA_NOPE and heads % 2 == 0
    BR = heads * vdim
    npair = heads // 2
    M = hb.shape[0]
    D = hb.shape[1]

    n_in = w_in.shape[1]
    n_pad = -n_in % V7X_LANES
    w_in_p = jnp.pad(w_in, ((0, 0), (0, n_pad))).astype(BF16)
    c_all = _matmul(hb, w_in_p, F32, "mla_in", tn_target=1152)
    n_all = n_in + n_pad

    tm = _pick_tile(T_pad, 1056, V7X_BF16_ROWS)
    nt = T_pad // tm

    half = MLA_ROPE // 2
    inv = ROPE_BASE ** (-jnp.arange(0, MLA_ROPE, 2, dtype=F32) / MLA_ROPE)
    ang = jnp.arange(T_pad, dtype=F32)[:, None] * inv[None, :]
    cos_t = jnp.tile(jnp.cos(ang), (1, 4))
    sin_t = jnp.tile(jnp.concatenate([-jnp.sin(ang), jnp.sin(ang)], axis=1), (1, 2))
    del half

    hd = MLA_NOPE + MLA_ROPE
    wq = w_uq.reshape(q_lora, npair, 2, hd)
    wq = jnp.concatenate([wq[..., 0, :MLA_NOPE], wq[..., 1, :MLA_NOPE],
                          wq[..., 0, MLA_NOPE:], wq[..., 1, MLA_NOPE:]], axis=-1)
    pw = 2 * hd
    wq = wq.reshape(q_lora, npair * pw).astype(BF16)
    cq_blk = BR // q_lora
    q = pl.pallas_call(
        _qproj_body,
        grid=(M // tm, npair),
        in_specs=[pl.BlockSpec((tm, q_lora), lambda i, j: (i, cq_blk)),
                  pl.BlockSpec((1, q_lora), lambda i, j: (0, 0)),
                  pl.BlockSpec((q_lora, pw), lambda i, j: (0, j)),
                  pl.BlockSpec((tm, V7X_LANES), lambda i, j: (i % nt, 0)),
                  pl.BlockSpec((tm, V7X_LANES), lambda i, j: (i % nt, 0))],
        out_specs=pl.BlockSpec((tm, pw), lambda i, j: (i, j)),
        out_shape=jax.ShapeDtypeStruct((M, npair * pw), BF16),
        scratch_shapes=[pltpu.VMEM((tm, q_lora), BF16)],
        compiler_params=_params(("parallel", "arbitrary"),
                                _nbytes((tm, q_lora), F32) + _nbytes((q_lora, pw), BF16)
                                + 2 * _nbytes((tm, pw), F32)),
        name="mla_qproj",
    )(c_all, q_norm.reshape(1, q_lora), wq, cos_t, sin_t)

    wkv = w_ukv.reshape(kv_lora, heads, MLA_NOPE + vdim)
    wkv = jnp.concatenate([wkv[..., :MLA_NOPE].reshape(kv_lora, BR),
                           wkv[..., MLA_NOPE:].reshape(kv_lora, BR)], axis=1).astype(BF16)
    tn = _pick_tile(2 * BR, 1024, V7X_LANES)
    ckv_blk = (BR + q_lora) // kv_lora
    kr_blk = (BR + q_lora + kv_lora) // V7X_LANES
    kv, krd = pl.pallas_call(
        _kvproj_body,
        grid=(M // tm, 2 * BR // tn),
        in_specs=[pl.BlockSpec((tm, kv_lora), lambda i, j: (i, ckv_blk)),
                  pl.BlockSpec((tm, V7X_LANES), lambda i, j: (i, kr_blk)),
                  pl.BlockSpec((1, kv_lora), lambda i, j: (0, 0)),
                  pl.BlockSpec((kv_lora, tn), lambda i, j: (0, j)),
                  pl.BlockSpec((tm, V7X_LANES), lambda i, j: (i % nt, 0)),
                  pl.BlockSpec((tm, V7X_LANES), lambda i, j: (i % nt, 0))],
        out_specs=[pl.BlockSpec((tm, tn), lambda i, j: (i, j)),
                   pl.BlockSpec((tm, V7X_LANES), lambda i, j: (i, 0))],
        out_shape=[jax.ShapeDtypeStruct((M, 2 * BR), BF16),
                   jax.ShapeDtypeStruct((M, V7X_LANES), BF16)],
        scratch_shapes=[pltpu.VMEM((tm, kv_lora), BF16)],
        compiler_params=_params(("parallel", "arbitrary"),
                                _nbytes((tm, kv_lora), F32) + _nbytes((kv_lora, tn), BF16)
                                + 2 * _nbytes((tm, tn), F32)),
        name="mla_kvproj",
    )(c_all, c_all, kv_norm.reshape(1, kv_lora), wkv, cos_t, sin_t)

    tq = _pick_tile(T_pad, 384, V7X_BF16_ROWS)
    pv = 2 * vdim
    q3 = q.reshape(B, T_pad, npair * pw)
    kv3 = kv.reshape(B, T_pad, 2 * BR)
    krd3 = krd.reshape(B, T_pad, V7X_LANES)
    c3 = c_all.reshape(B, T_pad, n_all)
    body = functools.partial(_attn_body, tk=tq, scale=float(hd) ** -0.5)
    blkbytes = (_nbytes((tq, pw), BF16) + 2 * _nbytes((T_pad, pv), BF16) + _nbytes((T_pad, V7X_LANES), BF16)
                + _nbytes((tq, pv), F32) + _nbytes((tq, pv), BF16) + 4 * _nbytes((tq, tq), F32))
    return pl.pallas_call(
        body,
        grid=(B, npair, T_pad // tq),
        in_specs=[pl.BlockSpec((1, tq, pw), lambda b, p, i: (b, i, p)),
                  pl.BlockSpec((1, T_pad, pv), lambda b, p, i: (b, 0, p)),
                  pl.BlockSpec((1, T_pad, pv), lambda b, p, i: (b, 0, p + npair)),
                  pl.BlockSpec((1, T_pad, V7X_LANES), lambda b, p, i: (b, 0, 0)),
                  pl.BlockSpec((1, tq, pv), lambda b, p, i: (b, i, p))],
        out_specs=pl.BlockSpec((1, tq, pv), lambda b, p, i: (b, i, p)),
        out_shape=jax.ShapeDtypeStruct((B, T_pad, BR), BF16),
        scratch_shapes=[pltpu.VMEM((2, tq, 1), F32),
                        pltpu.VMEM((2, tq, 1), F32),
                        pltpu.VMEM((2, tq, vdim), F32)],
        compiler_params=_params(("parallel", "parallel", "arbitrary"), blkbytes),
        name="mla_attn",
    )(q3, kv3, kv3, krd3, c3).reshape(M, BR)


def _rope_halves(x, cos, sin):
    hd = x.shape[1] // 2
    x1 = x[:, :hd]
    x2 = x[:, hd:]
    return jnp.concatenate([x1 * cos - x2 * sin, x2 * cos + x1 * sin], axis=1)


def _ret_body(lg_ref, q_ref, k_ref, v_ref, g_ref, cos_ref, sin_ref, y_ref, st_ref, *, kscale):
    c = pl.program_id(2)
    C = q_ref.shape[1]

    @pl.when(c == 0)
    def _():
        st_ref[...] = jnp.zeros_like(st_ref)

    lg = lg_ref[0, :, 0:1]
    cos = cos_ref[...]
    sin = sin_ref[...]
    qf = _rope_halves(q_ref[0], cos, sin)
    kf = _rope_halves(k_ref[0], cos, sin) * kscale
    qb = qf.astype(BF16)
    vb = v_ref[0].astype(BF16)

    ri = lax.broadcasted_iota(jnp.int32, (C, C), 0)
    ci = lax.broadcasted_iota(jnp.int32, (C, C), 1)
    diff = (ri - ci).astype(F32)
    decay = jnp.where(diff >= 0, jnp.exp(jnp.maximum(diff, 0.0) * lg), 0.0)
    s = lax.dot_general(qb, kf.astype(BF16), (((1,), (1,)), ((), ())), preferred_element_type=F32)
    intra = jnp.dot((s * decay).astype(BF16), vb, preferred_element_type=F32)

    jcol = lax.broadcasted_iota(jnp.int32, (C, 1), 0).astype(F32)
    state = st_ref[...]
    inter = jnp.dot(qb, state.astype(BF16), preferred_element_type=F32) * jnp.exp((jcol + 1.0) * lg)
    kw = (kf * jnp.exp((C - 1.0 - jcol) * lg)).astype(BF16)
    st_ref[...] = jnp.exp(C * lg) * state + lax.dot_general(
        kw, vb, (((0,), (0,)), ((), ())), preferred_element_type=F32)

    o = intra + inter
    mu = jnp.mean(o, axis=-1, keepdims=True)
    var = jnp.mean(jnp.square(o - mu), axis=-1, keepdims=True)
    o = (o - mu) * lax.rsqrt(var + LN_EPS)
    y_ref[0] = (o * _silu(g_ref[0])).astype(BF16)


def _ret_mix(qkvg, heads, B, T_pad):
    BR = qkvg.shape[1] // 4
    dk = BR // heads
    C = _pick_tile(T_pad, 384, V7X_BF16_ROWS)
    x3 = qkvg.reshape(B, T_pad, 4 * BR)
    hd = dk // 2
    inv = ROPE_BASE ** (-jnp.arange(0, dk, 2, dtype=F32) / dk)
    ang = jnp.arange(T_pad, dtype=F32)[:, None] * inv[None, :]
    cos_t = jnp.cos(ang)
    sin_t = jnp.sin(ang)
    log_g = jnp.log(1.0 - 2.0 ** (-5.0 - jnp.arange(heads, dtype=F32)))
    lg = jnp.broadcast_to(log_g[:, None, None], (heads, 1, V7X_LANES))
    body = functools.partial(_ret_body, kscale=float(dk) ** -0.5)
    blkbytes = 4 * _nbytes((C, dk), F32) + _nbytes((C, dk), BF16) + 6 * _nbytes((C, C), F32)
    return pl.pallas_call(
        body,
        grid=(B, heads, T_pad // C),
        in_specs=[pl.BlockSpec((1, 1, V7X_LANES), lambda b, h, c: (h, 0, 0)),
                  pl.BlockSpec((1, C, dk), lambda b, h, c: (b, c, h)),
                  pl.BlockSpec((1, C, dk), lambda b, h, c: (b, c, h + heads)),
                  pl.BlockSpec((1, C, dk), lambda b, h, c: (b, c, h + 2 * heads)),
                  pl.BlockSpec((1, C, dk), lambda b, h, c: (b, c, h + 3 * heads)),
                  pl.BlockSpec((C, hd), lambda b, h, c: (c, 0)),
                  pl.BlockSpec((C, hd), lambda b, h, c: (c, 0))],
        out_specs=pl.BlockSpec((1, C, dk), lambda b, h, c: (b, c, h)),
        out_shape=jax.ShapeDtypeStruct((B, T_pad, BR), BF16),
        scratch_shapes=[pltpu.VMEM((dk, dk), F32)],
        compiler_params=_params(("parallel", "parallel", "arbitrary"), blkbytes),
        name="ret_mix",
    )(lg, x3, x3, x3, x3, cos_t, sin_t).reshape(B * T_pad, BR)


RET_HEADS = 16


def kernel(x, meta_tokens, l0_w_in, l0_conv_w, l0_conv_b, l0_w_a, l0_b_a, l0_w_x, l0_b_x, l0_lam, l0_w_out, l0_ln_g, l0_ln_b, l1_w_in, l1_w_grp, l1_scale, l1_w_out, l1_ln_g, l1_ln_b, l2_w_in, l2_q_norm, l2_w_uq, l2_kv_norm, l2_w_ukv, l2_w_out, l2_ln_g, l2_ln_b, l3_w_in, l3_w_out, l3_ln_g, l3_ln_b):
    B, S, D = x.shape
    n_meta = meta_tokens.shape[0]
    T = n_meta + S
    T_pad = -(-T // TIME_ALIGN) * TIME_ALIGN
    M = B * T_pad

    meta = jnp.broadcast_to(meta_tokens.astype(x.dtype)[None], (B, n_meta, D))
    h = jnp.concatenate([meta, x, jnp.zeros((B, T_pad - T, D), x.dtype)], axis=1).reshape(M, D)
    hb = h.astype(BF16)

    ug = _matmul(hb, l0_w_in.astype(BF16), F32, "l0_in")
    y = _rglru_mix(ug, l0_conv_w, l0_conv_b, l0_w_a, l0_b_a, l0_w_x, l0_b_x, l0_lam, B, T_pad)
    h, hb = _mm_res_ln(y.reshape(M, -1), l0_w_out.astype(BF16), h, l0_ln_g, l0_ln_b, "l0_out")

    ug = _matmul(hb, l1_w_in.astype(BF16), F32, "l1_in")
    y = _pool_mix(ug, l1_w_grp, l1_scale, B, T_pad)
    h, hb = _mm_res_ln(y.reshape(M, -1), l1_w_out.astype(BF16), h, l1_ln_g, l1_ln_b, "l1_out")

    y = _mla_mix(hb, l2_w_in, l2_q_norm, l2_w_uq, l2_kv_norm, l2_w_ukv, B, T_pad)
    h, hb = _mm_res_ln(y, l2_w_out.astype(BF16), h, l2_ln_g, l2_ln_b, "l2_out")

    qkvg = _matmul(hb, l3_w_in.astype(BF16), F32, "l3_in")
    y = _ret_mix(qkvg, RET_HEADS, B, T_pad)
    h, hb = _mm_res_ln(y, l3_w_out.astype(BF16), h, l3_ln_g, l3_ln_b, "l3_out")

    return h.reshape(B, T_pad, D)[:, n_meta:T]
```

```python
import functools

import jax
import jax.numpy as jnp
from jax import lax
from jax.experimental import pallas as pl
from jax.experimental.pallas import tpu as pltpu

F32 = jnp.float32
BF16 = jnp.bfloat16

DEPTH = 4
ALPHA = (2.0 * DEPTH) ** 0.25
LN_EPS = 1e-5
RMS_EPS = 1e-6
ROPE_BASE = 10000.0
LRU_C = 8.0
POOL_WINDOWS = (2, 4, 8, 16)
MLA_NOPE = 128
MLA_ROPE = 64
LOG2_E = 1.4426950408889634

V7X_LANES = 128
V7X_SUBLANES = 8
V7X_BF16_ROWS = 16
V7X_SCOPED_VMEM_MAX = 60000 * 1024
V7X_VMEM_INTERNAL = 12 * 1024 * 1024

TIME_ALIGN = 128


def _pick_tile(n, target, mult):
    best = None
    for d in range(mult, min(n, target) + 1, mult):
        if n % d == 0:
            best = d
    return n if best is None else best


def _params(semantics, block_bytes):
    limit = min(2 * block_bytes + V7X_VMEM_INTERNAL, V7X_SCOPED_VMEM_MAX)
    return pltpu.CompilerParams(dimension_semantics=semantics, vmem_limit_bytes=int(limit))


def _nbytes(shape, dtype):
    n = 1
    for s in shape:
        n *= s
    return n * jnp.dtype(dtype).itemsize


def _sigmoid(x):
    return 1.0 / (1.0 + jnp.exp(-x))


def _silu(x):
    return x * _sigmoid(x)


def _mm_body(a_ref, w_ref, o_ref):
    o_ref[...] = jnp.dot(a_ref[...], w_ref[...], preferred_element_type=F32).astype(o_ref.dtype)


def _matmul(a, w, out_dtype, name, tm_target=1056, tn_target=1024):
    M, K = a.shape
    N = w.shape[1]
    tm = _pick_tile(M, tm_target, V7X_BF16_ROWS)
    tn = _pick_tile(N, tn_target, V7X_LANES)
    blk = _nbytes((tm, K), a.dtype) + _nbytes((K, tn), w.dtype) + _nbytes((tm, tn), out_dtype)
    return pl.pallas_call(
        _mm_body,
        grid=(M // tm, N // tn),
        in_specs=[pl.BlockSpec((tm, K), lambda i, j: (i, 0)),
                  pl.BlockSpec((K, tn), lambda i, j: (0, j))],
        out_specs=pl.BlockSpec((tm, tn), lambda i, j: (i, j)),
        out_shape=jax.ShapeDtypeStruct((M, N), out_dtype),
        compiler_params=_params(("parallel", "arbitrary"), blk),
        name=name,
    )(a, w)


def _mm_res_body(y_ref, w_ref, h_ref, o_ref):
    o_ref[...] = ALPHA * h_ref[...] + jnp.dot(y_ref[...], w_ref[...], preferred_element_type=F32)


def _mm_res(y, w, h, name, tm_target=1056, tn_target=512):
    M, K = y.shape
    N = w.shape[1]
    tm = _pick_tile(M, tm_target, V7X_BF16_ROWS)
    tn = _pick_tile(N, tn_target, V7X_LANES)
    blk = _nbytes((tm, K), BF16) + _nbytes((K, tn), BF16) + 2 * _nbytes((tm, tn), F32)
    return pl.pallas_call(
        _mm_res_body,
        grid=(M // tm, N // tn),
        in_specs=[pl.BlockSpec((tm, K), lambda i, j: (i, 0)),
                  pl.BlockSpec((K, tn), lambda i, j: (0, j)),
                  pl.BlockSpec((tm, tn), lambda i, j: (i, j))],
        out_specs=pl.BlockSpec((tm, tn), lambda i, j: (i, j)),
        out_shape=jax.ShapeDtypeStruct((M, N), F32),
        compiler_params=_params(("parallel", "arbitrary"), blk),
        name=name,
    )(y, w, h)


LN_ROWS = 16
LN_UNROLL = 4


def _row_sum(x):
    w = x.shape[1]
    while w > V7X_LANES and w % (2 * V7X_LANES) == 0:
        w //= 2
        x = x[:, :w] + x[:, w:]
    return jnp.sum(x, axis=1, keepdims=True)


def _ln_body(z_ref, g_ref, b_ref, oh_ref, ohb_ref, mu_ref, rs_ref):
    tr, D = z_ref.shape
    assert tr % (LN_ROWS * LN_UNROLL) == 0
    inv_d = 1.0 / D
    n = tr // LN_ROWS
    stat = (LN_ROWS, V7X_LANES)

    def rows_of(r):
        return pl.ds(pl.multiple_of(r * LN_ROWS, LN_ROWS), LN_ROWS)

    def mean_pass(r, carry):
        rows = rows_of(r)
        mu_ref[rows, :] = jnp.broadcast_to(_row_sum(z_ref[rows, :]) * inv_d, stat)
        return carry

    def var_pass(r, carry):
        rows = rows_of(r)
        d = z_ref[rows, :] - mu_ref[rows, 0:1]
        rs_ref[rows, :] = jnp.broadcast_to(lax.rsqrt(_row_sum(d * d) * inv_d + LN_EPS), stat)
        return carry

    def norm_pass(r, carry):
        rows = rows_of(r)
        o = (z_ref[rows, :] - mu_ref[rows, 0:1]) * rs_ref[rows, 0:1] * g_ref[...] + b_ref[...]
        oh_ref[rows, :] = o
        ohb_ref[rows, :] = o.astype(BF16)
        return carry

    lax.fori_loop(0, n, mean_pass, 0, unroll=LN_UNROLL)
    lax.fori_loop(0, n, var_pass, 0, unroll=LN_UNROLL)
    lax.fori_loop(0, n, norm_pass, 0)


def _layer_norm(z, gamma, beta, name, tr_target=256):
    M, D = z.shape
    tr = _pick_tile(M, tr_target, LN_ROWS * LN_UNROLL)
    blk = 2 * _nbytes((tr, D), F32) + _nbytes((tr, D), BF16)
    return pl.pallas_call(
        _ln_body,
        grid=(M // tr,),
        scratch_shapes=[pltpu.VMEM((tr, V7X_LANES), F32), pltpu.VMEM((tr, V7X_LANES), F32)],
        in_specs=[pl.BlockSpec((tr, D), lambda i: (i, 0)),
                  pl.BlockSpec((1, D), lambda i: (0, 0)),
                  pl.BlockSpec((1, D), lambda i: (0, 0))],
        out_specs=[pl.BlockSpec((tr, D), lambda i: (i, 0)),
                   pl.BlockSpec((tr, D), lambda i: (i, 0))],
        out_shape=[jax.ShapeDtypeStruct((M, D), F32), jax.ShapeDtypeStruct((M, D), BF16)],
        compiler_params=_params(("parallel",), blk),
        name=name,
    )(z, gamma.reshape(1, D), beta.reshape(1, D))


def _mm_res_ln(y, w, h, gamma, beta, name):
    return _layer_norm(_mm_res(y, w, h, name + "_mm"), gamma, beta, name + "_ln")


def _shift_rows(v, d, fill, row):
    n = v.shape[0]
    if d % V7X_SUBLANES == 0:
        return jnp.concatenate([jnp.full((d,) + v.shape[1:], fill, v.dtype), v[:n - d]], axis=0)
    return jnp.where(row >= d, pltpu.roll(v, d, 0), fill)


CONV_HIST = 8
SCAN_ROWS = 32


def _rglru_body(u_ref, g_ref, cw_ref, cb_ref, wg_ref, bg_ref, lam_ref, y_ref,
                ubuf, hcar, sa, sx, *, conv_w, hpb, blk):
    t = pl.program_id(2)
    tt = u_ref.shape[1]
    cw = u_ref.shape[2]

    @pl.when(t == 0)
    def _():
        ubuf[0:CONV_HIST, :] = jnp.zeros((CONV_HIST, cw), F32)
        hcar[...] = jnp.zeros_like(hcar)

    ubuf[CONV_HIST:CONV_HIST + tt, :] = u_ref[0]
    uc = cb_ref[...] + cw_ref[conv_w - 1:conv_w, :] * ubuf[CONV_HIST:CONV_HIST + tt, :]
    for k in range(conv_w - 1):
        off = CONV_HIST - (conv_w - 1) + k
        uc = uc + cw_ref[k:k + 1, :] * ubuf[off:off + tt, :]
    ubuf[0:CONV_HIST, :] = ubuf[tt:tt + CONV_HIST, :]

    lam = lam_ref[...]
    logsig = -(jnp.maximum(-lam, 0.0) + jnp.log1p(jnp.exp(-jnp.abs(lam))))
    for hh in range(hpb):
        cols = slice(hh * blk, (hh + 1) * blk)
        ucs = uc[:, cols]
        gate = jnp.dot(ucs.astype(BF16), wg_ref[hh], preferred_element_type=F32) + bg_ref[hh]
        r = _sigmoid(gate[:, :blk])
        i = _sigmoid(gate[:, blk:])
        a = jnp.exp(LRU_C * r * logsig[:, cols])
        sa[:, cols] = a
        sx[:, cols] = (ucs * i) * jnp.sqrt(1.0 - a * a)

    row = lax.broadcasted_iota(jnp.int32, (SCAN_ROWS, cw), 0)
    h_in = hcar[...]
    for s in range(tt // SCAN_ROWS):
        rows = slice(s * SCAN_ROWS, (s + 1) * SCAN_ROWS)
        a = sa[rows, :]
        x = sx[rows, :]
        d = 1
        while d < SCAN_ROWS:
            x = a * _shift_rows(x, d, 0.0, row) + x
            a = a * _shift_rows(a, d, 1.0, row)
            d *= 2
        h = x + a * h_in
        h_in = h[SCAN_ROWS - 1:SCAN_ROWS, :]
        y_ref[0, rows, :] = (h * _silu(g_ref[0, rows, :])).astype(BF16)
    hcar[...] = h_in


def _rglru_mix(ug, conv_w, conv_b, w_a, b_a, w_x, b_x, lam, B, T_pad):
    BR = conv_b.shape[0]
    heads, blk, _ = w_a.shape
    K = conv_w.shape[0]
    hpb = 2 if heads % 2 == 0 else 1
    cw = hpb * blk
    ncb = BR // cw
    tt = _pick_tile(T_pad, 384, SCAN_ROWS)
    wg = jnp.concatenate([w_a, w_x], axis=-1).astype(BF16)
    bg = jnp.concatenate([b_a.reshape(heads, 1, blk), b_x.reshape(heads, 1, blk)], axis=-1)
    ug3 = ug.reshape(B, T_pad, 2 * BR)
    body = functools.partial(_rglru_body, conv_w=K, hpb=hpb, blk=blk)
    blkbytes = 3 * _nbytes((tt, cw), F32) + _nbytes((hpb, blk, 2 * blk), BF16) + 2 * _nbytes((tt, cw), F32)
    return pl.pallas_call(
        body,
        grid=(B, ncb, T_pad // tt),
        in_specs=[pl.BlockSpec((1, tt, cw), lambda b, c, t: (b, t, c)),
                  pl.BlockSpec((1, tt, cw), lambda b, c, t: (b, t, c + ncb)),
                  pl.BlockSpec((K, cw), lambda b, c, t: (0, c)),
                  pl.BlockSpec((1, cw), lambda b, c, t: (0, c)),
                  pl.BlockSpec((hpb, blk, 2 * blk), lambda b, c, t: (c, 0, 0)),
                  pl.BlockSpec((hpb, 1, 2 * blk), lambda b, c, t: (c, 0, 0)),
                  pl.BlockSpec((1, cw), lambda b, c, t: (0, c))],
        out_specs=pl.BlockSpec((1, tt, cw), lambda b, c, t: (b, t, c)),
        out_shape=jax.ShapeDtypeStruct((B, T_pad, BR), BF16),
        scratch_shapes=[pltpu.VMEM((CONV_HIST + tt, cw), F32),
                        pltpu.VMEM((1, cw), F32),
                        pltpu.VMEM((tt, cw), F32),
                        pltpu.VMEM((tt, cw), F32)],
        compiler_params=_params(("parallel", "parallel", "arbitrary"), blkbytes),
        name="rglru_mix",
    )(ug3, ug3, conv_w.reshape(K, BR), conv_b.reshape(1, BR), wg, bg, lam.reshape(1, BR))


POOL_HIST = 24


def _pool_body(u_ref, g_ref, w_ref, sc_ref, y_ref, ubuf, sbuf, pbuf, *, windows):
    grp = pl.program_id(0)
    t = pl.program_id(2)
    tt = u_ref.shape[1]
    gw = u_ref.shape[2]
    n = POOL_HIST + tt

    @pl.when(t == 0)
    def _():
        ubuf[0:POOL_HIST, :] = jnp.zeros((POOL_HIST, gw), F32)

    sbuf[0:V7X_SUBLANES, :] = jnp.zeros((V7X_SUBLANES, gw), F32)
    ubuf[POOL_HIST:n, :] = u_ref[0]
    pos = lax.broadcasted_iota(jnp.int32, (tt, 1), 0) + t * tt + 1

    for gi, w in enumerate(windows):
        @pl.when(grp == gi)
        def _(w=w):
            lo = V7X_SUBLANES
            s = ubuf[lo:n, :] + ubuf[lo - 1:n - 1, :]
            d = 2
            while d < w:
                sbuf[lo:n, :] = s
                s = sbuf[lo:n, :] + sbuf[lo - d:n - d, :]
                d *= 2
            wsum = s[POOL_HIST - lo:, :]
            cnt = jnp.minimum(pos, w).astype(F32)
            pbuf[...] = (wsum / cnt - u_ref[0]).astype(BF16)

    ubuf[0:POOL_HIST, :] = ubuf[tt:n, :]
    mixed = jnp.dot(pbuf[...], w_ref[0], preferred_element_type=F32) * sc_ref[...]
    y_ref[0] = (mixed * _silu(g_ref[0])).astype(BF16)


def _pool_mix(ug, w_grp, scale, B, T_pad):
    ngrp, gw, _ = w_grp.shape
    BR = ngrp * gw
    tt = _pick_tile(T_pad, 384, V7X_BF16_ROWS)
    ug3 = ug.reshape(B, T_pad, 2 * BR)
    body = functools.partial(_pool_body, windows=POOL_WINDOWS[:ngrp])
    blkbytes = 5 * _nbytes((tt, gw), F32) + _nbytes((gw, gw), BF16)
    return pl.pallas_call(
        body,
        grid=(ngrp, B, T_pad // tt),
        in_specs=[pl.BlockSpec((1, tt, gw), lambda g, b, t: (b, t, g)),
                  pl.BlockSpec((1, tt, gw), lambda g, b, t: (b, t, g + ngrp)),
                  pl.BlockSpec((1, gw, gw), lambda g, b, t: (g, 0, 0)),
                  pl.BlockSpec((1, gw), lambda g, b, t: (0, g))],
        out_specs=pl.BlockSpec((1, tt, gw), lambda g, b, t: (b, t, g)),
        out_shape=jax.ShapeDtypeStruct((B, T_pad, BR), BF16),
        scratch_shapes=[pltpu.VMEM((POOL_HIST + tt, gw), F32),
                        pltpu.VMEM((POOL_HIST + tt, gw), F32),
                        pltpu.VMEM((tt, gw), BF16)],
        compiler_params=_params(("parallel", "parallel", "arbitrary"), blkbytes),
        name="pool_mix",
    )(ug3, ug3, w_grp.astype(BF16), scale.reshape(1, BR))


def _rope_pairs(x, c_ref, s_ref):
    half = MLA_ROPE // 2
    lane = lax.broadcasted_iota(jnp.int32, x.shape, 1)
    partner = jnp.where(lane % MLA_ROPE < half,
                        pltpu.roll(x, V7X_LANES - half, 1), pltpu.roll(x, half, 1))
    return x * c_ref[...] + partner * s_ref[...]


def _rms_scale(x, g):
    return x * lax.rsqrt(jnp.mean(jnp.square(x), axis=-1, keepdims=True) + RMS_EPS) * g


def _qproj_body(c_ref, n_ref, w_ref, cos_ref, sin_ref, q_ref, xn_ref, *, ppb):
    @pl.when(pl.program_id(1) == 0)
    def _():
        xn_ref[...] = _rms_scale(c_ref[...], n_ref[...]).astype(BF16)

    res = jnp.dot(xn_ref[...], w_ref[...], preferred_element_type=F32)
    nn = 2 * MLA_NOPE
    pw = nn + 2 * MLA_ROPE
    for p in range(ppb):
        q_ref[:, p * pw:p * pw + nn] = res[:, p * pw:p * pw + nn].astype(BF16)
        q_ref[:, p * pw + nn:(p + 1) * pw] = _rope_pairs(
            res[:, p * pw + nn:(p + 1) * pw], cos_ref, sin_ref).astype(BF16)


def _kvproj_body(c_ref, kr_ref, n_ref, w_ref, cos_ref, sin_ref, kv_ref, krd_ref, xn_ref):
    @pl.when(pl.program_id(1) == 0)
    def _():
        xn_ref[...] = _rms_scale(c_ref[...], n_ref[...]).astype(BF16)
        rot = _rope_pairs(kr_ref[...], cos_ref, sin_ref)
        lane = lax.broadcasted_iota(jnp.int32, rot.shape, 1)
        krd_ref[...] = jnp.where(lane < MLA_ROPE, rot, pltpu.roll(rot, MLA_ROPE, 1)).astype(BF16)

    kv_ref[...] = jnp.dot(xn_ref[...], w_ref[...], preferred_element_type=F32).astype(BF16)


def _attn_body(q_ref, kn_ref, v_ref, kr_ref, g_ref, y_ref, m_ref, acc_ref, *, scale2):
    qi = pl.program_id(2)
    tq = q_ref.shape[1]
    wide = 2 * tq
    nn = 2 * MLA_NOPE
    lane = lax.broadcasted_iota(jnp.int32, (tq, V7X_LANES), 1)
    qr = q_ref[0, :, nn:]
    zero = jnp.zeros_like(qr)
    qs = (jnp.concatenate([q_ref[0, :, :MLA_NOPE], jnp.where(lane < MLA_ROPE, qr, zero)], axis=1),
          jnp.concatenate([q_ref[0, :, MLA_NOPE:nn], jnp.where(lane >= MLA_ROPE, qr, zero)], axis=1))

    m_ref[...] = jnp.full_like(m_ref, -jnp.inf)
    acc_ref[...] = jnp.zeros_like(acc_ref)

    def block(k0, tk, mask_shift):
        krope = kr_ref[0, pl.ds(k0, tk), :]
        ones = jnp.ones((tk, V7X_LANES), BF16)
        nc = tk // V7X_LANES
        for hh in range(2):
            hc = slice(hh * MLA_NOPE, (hh + 1) * MLA_NOPE)
            kf = jnp.concatenate([kn_ref[0, pl.ds(k0, tk), hc], krope], axis=1)
            s = lax.dot_general(qs[hh], kf, (((1,), (1,)), ((), ())),
                                preferred_element_type=F32) * scale2
            if mask_shift is not None:
                rq = lax.broadcasted_iota(jnp.int32, (tq, tk), 0)
                ck = lax.broadcasted_iota(jnp.int32, (tq, tk), 1)
                s = jnp.where(ck <= rq + mask_shift, s, -1e30)
            cols = [s[:, c * V7X_LANES:(c + 1) * V7X_LANES] for c in range(nc)]
            mloc = cols[0]
            for c in range(1, nc):
                mloc = jnp.maximum(mloc, cols[c])
            m_old = m_ref[hh]
            m_new = jnp.maximum(m_old, jnp.max(mloc, axis=1, keepdims=True))
            p = jnp.concatenate([jnp.exp2(cols[c] - m_new) for c in range(nc)], axis=1).astype(BF16)
            corr = jnp.exp2(m_old - m_new)
            vf = jnp.concatenate([v_ref[0, pl.ds(k0, tk), hc], ones], axis=1)
            pv = jnp.dot(p, vf, preferred_element_type=F32)
            acc_ref[hh] = jnp.concatenate([corr, corr], axis=1) * acc_ref[hh] + pv
            m_ref[hh] = m_new

    def loop_body(j, carry):
        block(pl.multiple_of(j * wide, wide), wide, None)
        return carry

    lax.fori_loop(0, qi // 2, loop_body, 0)

    @pl.when(qi % 2 == 1)
    def _():
        block(pl.multiple_of((qi - 1) * tq, tq), wide, tq)

    @pl.when(qi % 2 == 0)
    def _():
        block(pl.multiple_of(qi * tq, tq), tq, 0)

    gate = _silu(g_ref[0])
    for hh in range(2):
        hc = slice(hh * MLA_NOPE, (hh + 1) * MLA_NOPE)
        acc = acc_ref[hh]
        o = acc[:, :MLA_NOPE] / acc[:, MLA_NOPE:]
        y_ref[0, :, hc] = (o * gate[:, hc]).astype(BF16)


def _mla_mix(hb, w_in, q_norm, w_uq, kv_norm, w_ukv, B, T_pad):
    q_lora = q_norm.shape[0]
    kv_lora = kv_norm.shape[0]
    heads = w_uq.shape[1] // (MLA_NOPE + MLA_ROPE)
    vdim = w_ukv.shape[1] // heads - MLA_NOPE
    assert vdim == MLA_NOPE and heads % 2 == 0
    BR = heads * vdim
    npair = heads // 2
    M = hb.shape[0]
    D = hb.shape[1]

    n_in = w_in.shape[1]
    n_pad = -n_in % V7X_LANES
    w_in_p = jnp.pad(w_in, ((0, 0), (0, n_pad))).astype(BF16)
    c_all = _matmul(hb, w_in_p, F32, "mla_in", tn_target=1152)
    n_all = n_in + n_pad

    tm = _pick_tile(T_pad, 1056, V7X_BF16_ROWS)
    nt = T_pad // tm

    half = MLA_ROPE // 2
    inv = ROPE_BASE ** (-jnp.arange(0, MLA_ROPE, 2, dtype=F32) / MLA_ROPE)
    ang = jnp.arange(T_pad, dtype=F32)[:, None] * inv[None, :]
    cos_t = jnp.tile(jnp.cos(ang), (1, 4))
    sin_t = jnp.tile(jnp.concatenate([-jnp.sin(ang), jnp.sin(ang)], axis=1), (1, 2))
    del half

    hd = MLA_NOPE + MLA_ROPE
    wq = w_uq.reshape(q_lora, npair, 2, hd)
    wq = jnp.concatenate([wq[..., 0, :MLA_NOPE], wq[..., 1, :MLA_NOPE],
                          wq[..., 0, MLA_NOPE:], wq[..., 1, MLA_NOPE:]], axis=-1)
    pw = 2 * hd
    wq = wq.reshape(q_lora, npair * pw).astype(BF16)
    cq_blk = BR // q_lora
    ppb = _pick_tile(npair, 4, 1)
    qw = ppb * pw
    q = pl.pallas_call(
        functools.partial(_qproj_body, ppb=ppb),
        grid=(M // tm, npair // ppb),
        in_specs=[pl.BlockSpec((tm, q_lora), lambda i, j: (i, cq_blk)),
                  pl.BlockSpec((1, q_lora), lambda i, j: (0, 0)),
                  pl.BlockSpec((q_lora, qw), lambda i, j: (0, j)),
                  pl.BlockSpec((tm, V7X_LANES), lambda i, j: (i % nt, 0)),
                  pl.BlockSpec((tm, V7X_LANES), lambda i, j: (i % nt, 0))],
        out_specs=pl.BlockSpec((tm, qw), lambda i, j: (i, j)),
        out_shape=jax.ShapeDtypeStruct((M, npair * pw), BF16),
        scratch_shapes=[pltpu.VMEM((tm, q_lora), BF16)],
        compiler_params=_params(("parallel", "arbitrary"),
                                _nbytes((tm, q_lora), F32) + _nbytes((q_lora, qw), BF16)
                                + 2 * _nbytes((tm, qw), F32)),
        name="mla_qproj",
    )(c_all, q_norm.reshape(1, q_lora), wq, cos_t, sin_t)

    wkv = w_ukv.reshape(kv_lora, heads, MLA_NOPE + vdim)
    wkv = jnp.concatenate([wkv[..., :MLA_NOPE].reshape(kv_lora, BR),
                           wkv[..., MLA_NOPE:].reshape(kv_lora, BR)], axis=1).astype(BF16)
    tn = _pick_tile(2 * BR, 1024, V7X_LANES)
    ckv_blk = (BR + q_lora) // kv_lora
    kr_blk = (BR + q_lora + kv_lora) // V7X_LANES
    kv, krd = pl.pallas_call(
        _kvproj_body,
        grid=(M // tm, 2 * BR // tn),
        in_specs=[pl.BlockSpec((tm, kv_lora), lambda i, j: (i, ckv_blk)),
                  pl.BlockSpec((tm, V7X_LANES), lambda i, j: (i, kr_blk)),
                  pl.BlockSpec((1, kv_lora), lambda i, j: (0, 0)),
                  pl.BlockSpec((kv_lora, tn), lambda i, j: (0, j)),
                  pl.BlockSpec((tm, V7X_LANES), lambda i, j: (i % nt, 0)),
                  pl.BlockSpec((tm, V7X_LANES), lambda i, j: (i % nt, 0))],
        out_specs=[pl.BlockSpec((tm, tn), lambda i, j: (i, j)),
                   pl.BlockSpec((tm, V7X_LANES), lambda i, j: (i, 0))],
        out_shape=[jax.ShapeDtypeStruct((M, 2 * BR), BF16),
                   jax.ShapeDtypeStruct((M, V7X_LANES), BF16)],
        scratch_shapes=[pltpu.VMEM((tm, kv_lora), BF16)],
        compiler_params=_params(("parallel", "arbitrary"),
                                _nbytes((tm, kv_lora), F32) + _nbytes((kv_lora, tn), BF16)
                                + 2 * _nbytes((tm, tn), F32)),
        name="mla_kvproj",
    )(c_all, c_all, kv_norm.reshape(1, kv_lora), wkv, cos_t, sin_t)

    tq = _pick_tile(T_pad, 384, V7X_BF16_ROWS)
    pv = 2 * vdim
    q3 = q.reshape(B, T_pad, npair * pw)
    kv3 = kv.reshape(B, T_pad, 2 * BR)
    krd3 = krd.reshape(B, T_pad, V7X_LANES)
    c3 = c_all.reshape(B, T_pad, n_all)
    body = functools.partial(_attn_body, scale2=float(hd) ** -0.5 * LOG2_E)
    blkbytes = (_nbytes((tq, pw), BF16) + 2 * _nbytes((T_pad, pv), BF16) + _nbytes((T_pad, V7X_LANES), BF16)
                + _nbytes((tq, pv), F32) + _nbytes((tq, pv), BF16) + 4 * _nbytes((tq, 2 * tq), F32))
    return pl.pallas_call(
        body,
        grid=(B, npair, T_pad // tq),
        in_specs=[pl.BlockSpec((1, tq, pw), lambda b, p, i: (b, i, p)),
                  pl.BlockSpec((1, T_pad, pv), lambda b, p, i: (b, 0, p)),
                  pl.BlockSpec((1, T_pad, pv), lambda b, p, i: (b, 0, p + npair)),
                  pl.BlockSpec((1, T_pad, V7X_LANES), lambda b, p, i: (b, 0, 0)),
                  pl.BlockSpec((1, tq, pv), lambda b, p, i: (b, i, p))],
        out_specs=pl.BlockSpec((1, tq, pv), lambda b, p, i: (b, i, p)),
        out_shape=jax.ShapeDtypeStruct((B, T_pad, BR), BF16),
        scratch_shapes=[pltpu.VMEM((2, tq, V7X_LANES), F32),
                        pltpu.VMEM((2, tq, 2 * vdim), F32)],
        compiler_params=_params(("parallel", "parallel", "arbitrary"), blkbytes),
        name="mla_attn",
    )(q3, kv3, kv3, krd3, c3).reshape(M, BR)


def _rope_halves(x, cos, sin):
    hd = x.shape[1] // 2
    x1 = x[:, :hd]
    x2 = x[:, hd:]
    return jnp.concatenate([x1 * cos - x2 * sin, x2 * cos + x1 * sin], axis=1)


def _ret_body(lg_ref, q_ref, k_ref, v_ref, g_ref, cos_ref, sin_ref, y_ref, st_ref, dec_ref, *, kscale):
    c = pl.program_id(2)
    C = q_ref.shape[1]
    lg = lg_ref[0, :, 0:1]

    @pl.when(c == 0)
    def _():
        st_ref[...] = jnp.zeros_like(st_ref)
        ri = lax.broadcasted_iota(jnp.int32, (C, C), 0)
        ci = lax.broadcasted_iota(jnp.int32, (C, C), 1)
        diff = (ri - ci).astype(F32)
        dec_ref[...] = jnp.where(diff >= 0, jnp.exp(jnp.maximum(diff, 0.0) * lg), 0.0)

    cos = cos_ref[...]
    sin = sin_ref[...]
    qf = _rope_halves(q_ref[0], cos, sin)
    kf = _rope_halves(k_ref[0], cos, sin) * kscale
    qb = qf.astype(BF16)
    vb = v_ref[0].astype(BF16)

    s = lax.dot_general(qb, kf.astype(BF16), (((1,), (1,)), ((), ())), preferred_element_type=F32)
    intra = jnp.dot((s * dec_ref[...]).astype(BF16), vb, preferred_element_type=F32)

    jcol = lax.broadcasted_iota(jnp.int32, (C, 1), 0).astype(F32)
    state = st_ref[...]
    inter = jnp.dot(qb, state.astype(BF16), preferred_element_type=F32) * jnp.exp((jcol + 1.0) * lg)
    kw = (kf * jnp.exp((C - 1.0 - jcol) * lg)).astype(BF16)
    st_ref[...] = jnp.exp(C * lg) * state + lax.dot_general(
        kw, vb, (((0,), (0,)), ((), ())), preferred_element_type=F32)

    o = intra + inter
    mu = jnp.mean(o, axis=-1, keepdims=True)
    var = jnp.mean(jnp.square(o - mu), axis=-1, keepdims=True)
    o = (o - mu) * lax.rsqrt(var + LN_EPS)
    y_ref[0] = (o * _silu(g_ref[0])).astype(BF16)


def _ret_mix(qkvg, heads, B, T_pad):
    BR = qkvg.shape[1] // 4
    dk = BR // heads
    C = _pick_tile(T_pad, 384, V7X_BF16_ROWS)
    x3 = qkvg.reshape(B, T_pad, 4 * BR)
    hd = dk // 2
    inv = ROPE_BASE ** (-jnp.arange(0, dk, 2, dtype=F32) / dk)
    ang = jnp.arange(T_pad, dtype=F32)[:, None] * inv[None, :]
    cos_t = jnp.cos(ang)
    sin_t = jnp.sin(ang)
    log_g = jnp.log(1.0 - 2.0 ** (-5.0 - jnp.arange(heads, dtype=F32)))
    lg = jnp.broadcast_to(log_g[:, None, None], (heads, 1, V7X_LANES))
    body = functools.partial(_ret_body, kscale=float(dk) ** -0.5)
    blkbytes = 4 * _nbytes((C, dk), F32) + _nbytes((C, dk), BF16) + 6 * _nbytes((C, C), F32)
    return pl.pallas_call(
        body,
        grid=(B, heads, T_pad // C),
        in_specs=[pl.BlockSpec((1, 1, V7X_LANES), lambda b, h, c: (h, 0, 0)),
                  pl.BlockSpec((1, C, dk), lambda b, h, c: (b, c, h)),
                  pl.BlockSpec((1, C, dk), lambda b, h, c: (b, c, h + heads)),
                  pl.BlockSpec((1, C, dk), lambda b, h, c: (b, c, h + 2 * heads)),
                  pl.BlockSpec((1, C, dk), lambda b, h, c: (b, c, h + 3 * heads)),
                  pl.BlockSpec((C, hd), lambda b, h, c: (c, 0)),
                  pl.BlockSpec((C, hd), lambda b, h, c: (c, 0))],
        out_specs=pl.BlockSpec((1, C, dk), lambda b, h, c: (b, c, h)),
        out_shape=jax.ShapeDtypeStruct((B, T_pad, BR), BF16),
        scratch_shapes=[pltpu.VMEM((dk, dk), F32), pltpu.VMEM((C, C), F32)],
        compiler_params=_params(("parallel", "parallel", "arbitrary"), blkbytes),
        name="ret_mix",
    )(lg, x3, x3, x3, x3, cos_t, sin_t).reshape(B * T_pad, BR)


RET_HEADS = 16


def kernel(x, meta_tokens, l0_w_in, l0_conv_w, l0_conv_b, l0_w_a, l0_b_a, l0_w_x, l0_b_x, l0_lam, l0_w_out, l0_ln_g, l0_ln_b, l1_w_in, l1_w_grp, l1_scale, l1_w_out, l1_ln_g, l1_ln_b, l2_w_in, l2_q_norm, l2_w_uq, l2_kv_norm, l2_w_ukv, l2_w_out, l2_ln_g, l2_ln_b, l3_w_in, l3_w_out, l3_ln_g, l3_ln_b):
    B, S, D = x.shape
    n_meta = meta_tokens.shape[0]
    T = n_meta + S
    T_pad = -(-T // TIME_ALIGN) * TIME_ALIGN
    M = B * T_pad

    meta = jnp.broadcast_to(meta_tokens.astype(x.dtype)[None], (B, n_meta, D))
    h = jnp.concatenate([meta, x, jnp.zeros((B, T_pad - T, D), x.dtype)], axis=1).reshape(M, D)
    hb = h.astype(BF16)

    ug = _matmul(hb, l0_w_in.astype(BF16), F32, "l0_in")
    y = _rglru_mix(ug, l0_conv_w, l0_conv_b, l0_w_a, l0_b_a, l0_w_x, l0_b_x, l0_lam, B, T_pad)
    h, hb = _mm_res_ln(y.reshape(M, -1), l0_w_out.astype(BF16), h, l0_ln_g, l0_ln_b, "l0_out")

    ug = _matmul(hb, l1_w_in.astype(BF16), F32, "l1_in")
    y = _pool_mix(ug, l1_w_grp, l1_scale, B, T_pad)
    h, hb = _mm_res_ln(y.reshape(M, -1), l1_w_out.astype(BF16), h, l1_ln_g, l1_ln_b, "l1_out")

    y = _mla_mix(hb, l2_w_in, l2_q_norm, l2_w_uq, l2_kv_norm, l2_w_ukv, B, T_pad)
    h, hb = _mm_res_ln(y, l2_w_out.astype(BF16), h, l2_ln_g, l2_ln_b, "l2_out")

    qkvg = _matmul(hb, l3_w_in.astype(BF16), F32, "l3_in")
    y = _ret_mix(qkvg, RET_HEADS, B, T_pad)
    h, hb = _mm_res_ln(y, l3_w_out.astype(BF16), h, l3_ln_g, l3_ln_b, "l3_out")

    return h.reshape(B, T_pad, D)[:, n_meta:T]
```

```python
import functools

import jax
import jax.numpy as jnp
from jax import lax
from jax.experimental import pallas as pl
from jax.experimental.pallas import tpu as pltpu

F32 = jnp.float32
BF16 = jnp.bfloat16

DEPTH = 4
ALPHA = (2.0 * DEPTH) ** 0.25
LN_EPS = 1e-5
RMS_EPS = 1e-6
ROPE_BASE = 10000.0
LRU_C = 8.0
POOL_WINDOWS = (2, 4, 8, 16)
MLA_NOPE = 128
MLA_ROPE = 64
LOG2_E = 1.4426950408889634

V7X_LANES = 128
V7X_SUBLANES = 8
V7X_BF16_ROWS = 16
V7X_SCOPED_VMEM_MAX = 60000 * 1024
V7X_VMEM_INTERNAL = 12 * 1024 * 1024

TIME_ALIGN = 128


def _pick_tile(n, target, mult):
    best = None
    for d in range(mult, min(n, target) + 1, mult):
        if n % d == 0:
            best = d
    return n if best is None else best


def _params(semantics, block_bytes):
    limit = min(2 * block_bytes + V7X_VMEM_INTERNAL, V7X_SCOPED_VMEM_MAX)
    return pltpu.CompilerParams(dimension_semantics=semantics, vmem_limit_bytes=int(limit))


def _nbytes(shape, dtype):
    n = 1
    for s in shape:
        n *= s
    return n * jnp.dtype(dtype).itemsize


def _sigmoid(x):
    return 1.0 / (1.0 + jnp.exp(-x))


def _silu(x):
    return x * _sigmoid(x)


def _mm_body(a_ref, w_ref, o_ref):
    o_ref[...] = jnp.dot(a_ref[...], w_ref[...], preferred_element_type=F32).astype(o_ref.dtype)


def _matmul(a, w, out_dtype, name, tm_target=1056, tn_target=1024):
    M, K = a.shape
    N = w.shape[1]
    tm = _pick_tile(M, tm_target, V7X_BF16_ROWS)
    tn = _pick_tile(N, tn_target, V7X_LANES)
    blk = _nbytes((tm, K), a.dtype) + _nbytes((K, tn), w.dtype) + _nbytes((tm, tn), out_dtype)
    return pl.pallas_call(
        _mm_body,
        grid=(M // tm, N // tn),
        in_specs=[pl.BlockSpec((tm, K), lambda i, j: (i, 0)),
                  pl.BlockSpec((K, tn), lambda i, j: (0, j))],
        out_specs=pl.BlockSpec((tm, tn), lambda i, j: (i, j)),
        out_shape=jax.ShapeDtypeStruct((M, N), out_dtype),
        compiler_params=_params(("parallel", "arbitrary"), blk),
        name=name,
    )(a, w)


def _mm_res_body(y_ref, w_ref, h_ref, o_ref):
    o_ref[...] = ALPHA * h_ref[...] + jnp.dot(y_ref[...], w_ref[...], preferred_element_type=F32)


def _mm_res(y, w, h, name, tm_target=1056, tn_target=512):
    M, K = y.shape
    N = w.shape[1]
    tm = _pick_tile(M, tm_target, V7X_BF16_ROWS)
    tn = _pick_tile(N, tn_target, V7X_LANES)
    blk = _nbytes((tm, K), BF16) + _nbytes((K, tn), BF16) + 2 * _nbytes((tm, tn), F32)
    return pl.pallas_call(
        _mm_res_body,
        grid=(M // tm, N // tn),
        in_specs=[pl.BlockSpec((tm, K), lambda i, j: (i, 0)),
                  pl.BlockSpec((K, tn), lambda i, j: (0, j)),
                  pl.BlockSpec((tm, tn), lambda i, j: (i, j))],
        out_specs=pl.BlockSpec((tm, tn), lambda i, j: (i, j)),
        out_shape=jax.ShapeDtypeStruct((M, N), F32),
        compiler_params=_params(("parallel", "arbitrary"), blk),
        name=name,
    )(y, w, h)


LN_ROWS = 16
LN_UNROLL = 4


def _row_sum(x):
    w = x.shape[1]
    while w > V7X_LANES and w % (2 * V7X_LANES) == 0:
        w //= 2
        x = x[:, :w] + x[:, w:]
    return jnp.sum(x, axis=1, keepdims=True)


def _ln_body(z_ref, g_ref, b_ref, oh_ref, ohb_ref, mu_ref, rs_ref):
    tr, D = z_ref.shape
    assert tr % (LN_ROWS * LN_UNROLL) == 0
    inv_d = 1.0 / D
    n = tr // LN_ROWS
    stat = (LN_ROWS, V7X_LANES)

    def rows_of(r):
        return pl.ds(pl.multiple_of(r * LN_ROWS, LN_ROWS), LN_ROWS)

    def mean_pass(r, carry):
        rows = rows_of(r)
        mu_ref[rows, :] = jnp.broadcast_to(_row_sum(z_ref[rows, :]) * inv_d, stat)
        return carry

    def var_pass(r, carry):
        rows = rows_of(r)
        d = z_ref[rows, :] - mu_ref[rows, 0:1]
        rs_ref[rows, :] = jnp.broadcast_to(lax.rsqrt(_row_sum(d * d) * inv_d + LN_EPS), stat)
        return carry

    def norm_pass(r, carry):
        rows = rows_of(r)
        o = (z_ref[rows, :] - mu_ref[rows, 0:1]) * rs_ref[rows, 0:1] * g_ref[...] + b_ref[...]
        oh_ref[rows, :] = o
        ohb_ref[rows, :] = o.astype(BF16)
        return carry

    lax.fori_loop(0, n, mean_pass, 0, unroll=LN_UNROLL)
    lax.fori_loop(0, n, var_pass, 0, unroll=LN_UNROLL)
    lax.fori_loop(0, n, norm_pass, 0)


def _layer_norm(z, gamma, beta, name, tr_target=256):
    M, D = z.shape
    tr = _pick_tile(M, tr_target, LN_ROWS * LN_UNROLL)
    blk = 2 * _nbytes((tr, D), F32) + _nbytes((tr, D), BF16)
    return pl.pallas_call(
        _ln_body,
        grid=(M // tr,),
        scratch_shapes=[pltpu.VMEM((tr, V7X_LANES), F32), pltpu.VMEM((tr, V7X_LANES), F32)],
        in_specs=[pl.BlockSpec((tr, D), lambda i: (i, 0)),
                  pl.BlockSpec((1, D), lambda i: (0, 0)),
                  pl.BlockSpec((1, D), lambda i: (0, 0))],
        out_specs=[pl.BlockSpec((tr, D), lambda i: (i, 0)),
                   pl.BlockSpec((tr, D), lambda i: (i, 0))],
        out_shape=[jax.ShapeDtypeStruct((M, D), F32), jax.ShapeDtypeStruct((M, D), BF16)],
        compiler_params=_params(("parallel",), blk),
        name=name,
    )(z, gamma.reshape(1, D), beta.reshape(1, D))


def _mm_res_ln(y, w, h, gamma, beta, name):
    return _layer_norm(_mm_res(y, w, h, name + "_mm"), gamma, beta, name + "_ln")


def _shift_rows(v, d, fill, row):
    n = v.shape[0]
    if d % V7X_SUBLANES == 0:
        return jnp.concatenate([jnp.full((d,) + v.shape[1:], fill, v.dtype), v[:n - d]], axis=0)
    return jnp.where(row >= d, pltpu.roll(v, d, 0), fill)


CONV_HIST = 8
SCAN_ROWS = 32


def _rglru_body(u_ref, g_ref, cw_ref, cb_ref, wg_ref, bg_ref, lam_ref, y_ref,
                ubuf, hcar, sa, sx, *, conv_w, hpb, blk):
    t = pl.program_id(2)
    tt = u_ref.shape[1]
    cw = u_ref.shape[2]

    @pl.when(t == 0)
    def _():
        ubuf[0:CONV_HIST, :] = jnp.zeros((CONV_HIST, cw), F32)
        hcar[...] = jnp.zeros_like(hcar)

    ubuf[CONV_HIST:CONV_HIST + tt, :] = u_ref[0]
    uc = cb_ref[...] + cw_ref[conv_w - 1:conv_w, :] * ubuf[CONV_HIST:CONV_HIST + tt, :]
    for k in range(conv_w - 1):
        off = CONV_HIST - (conv_w - 1) + k
        uc = uc + cw_ref[k:k + 1, :] * ubuf[off:off + tt, :]
    ubuf[0:CONV_HIST, :] = ubuf[tt:tt + CONV_HIST, :]

    lam = lam_ref[...]
    logsig = -(jnp.maximum(-lam, 0.0) + jnp.log1p(jnp.exp(-jnp.abs(lam))))
    for hh in range(hpb):
        cols = slice(hh * blk, (hh + 1) * blk)
        ucs = uc[:, cols]
        gate = jnp.dot(ucs.astype(BF16), wg_ref[hh], preferred_element_type=F32) + bg_ref[hh]
        r = _sigmoid(gate[:, :blk])
        i = _sigmoid(gate[:, blk:])
        a = jnp.exp(LRU_C * r * logsig[:, cols])
        sa[:, cols] = a
        sx[:, cols] = (ucs * i) * jnp.sqrt(1.0 - a * a)

    row = lax.broadcasted_iota(jnp.int32, (SCAN_ROWS, cw), 0)
    h_in = hcar[...]
    for s in range(tt // SCAN_ROWS):
        rows = slice(s * SCAN_ROWS, (s + 1) * SCAN_ROWS)
        a = sa[rows, :]
        x = sx[rows, :]
        d = 1
        while d < SCAN_ROWS:
            x = a * _shift_rows(x, d, 0.0, row) + x
            a = a * _shift_rows(a, d, 1.0, row)
            d *= 2
        h = x + a * h_in
        h_in = h[SCAN_ROWS - 1:SCAN_ROWS, :]
        y_ref[0, rows, :] = (h * _silu(g_ref[0, rows, :])).astype(BF16)
    hcar[...] = h_in


def _rglru_mix(ug, conv_w, conv_b, w_a, b_a, w_x, b_x, lam, B, T_pad):
    BR = conv_b.shape[0]
    heads, blk, _ = w_a.shape
    K = conv_w.shape[0]
    hpb = 2 if heads % 2 == 0 else 1
    cw = hpb * blk
    ncb = BR // cw
    tt = _pick_tile(T_pad, 384, SCAN_ROWS)
    wg = jnp.concatenate([w_a, w_x], axis=-1).astype(BF16)
    bg = jnp.concatenate([b_a.reshape(heads, 1, blk), b_x.reshape(heads, 1, blk)], axis=-1)
    ug3 = ug.reshape(B, T_pad, 2 * BR)
    body = functools.partial(_rglru_body, conv_w=K, hpb=hpb, blk=blk)
    blkbytes = 3 * _nbytes((tt, cw), F32) + _nbytes((hpb, blk, 2 * blk), BF16) + 2 * _nbytes((tt, cw), F32)
    return pl.pallas_call(
        body,
        grid=(B, ncb, T_pad // tt),
        in_specs=[pl.BlockSpec((1, tt, cw), lambda b, c, t: (b, t, c)),
                  pl.BlockSpec((1, tt, cw), lambda b, c, t: (b, t, c + ncb)),
                  pl.BlockSpec((K, cw), lambda b, c, t: (0, c)),
                  pl.BlockSpec((1, cw), lambda b, c, t: (0, c)),
                  pl.BlockSpec((hpb, blk, 2 * blk), lambda b, c, t: (c, 0, 0)),
                  pl.BlockSpec((hpb, 1, 2 * blk), lambda b, c, t: (c, 0, 0)),
                  pl.BlockSpec((1, cw), lambda b, c, t: (0, c))],
        out_specs=pl.BlockSpec((1, tt, cw), lambda b, c, t: (b, t, c)),
        out_shape=jax.ShapeDtypeStruct((B, T_pad, BR), BF16),
        scratch_shapes=[pltpu.VMEM((CONV_HIST + tt, cw), F32),
                        pltpu.VMEM((1, cw), F32),
                        pltpu.VMEM((tt, cw), F32),
                        pltpu.VMEM((tt, cw), F32)],
        compiler_params=_params(("parallel", "parallel", "arbitrary"), blkbytes),
        name="rglru_mix",
    )(ug3, ug3, conv_w.reshape(K, BR), conv_b.reshape(1, BR), wg, bg, lam.reshape(1, BR))


POOL_HIST = 24


def _pool_body(u_ref, g_ref, w_ref, sc_ref, y_ref, ubuf, sbuf, pbuf, *, windows):
    grp = pl.program_id(0)
    t = pl.program_id(2)
    tt = u_ref.shape[1]
    gw = u_ref.shape[2]
    n = POOL_HIST + tt

    @pl.when(t == 0)
    def _():
        ubuf[0:POOL_HIST, :] = jnp.zeros((POOL_HIST, gw), F32)

    sbuf[0:V7X_SUBLANES, :] = jnp.zeros((V7X_SUBLANES, gw), F32)
    ubuf[POOL_HIST:n, :] = u_ref[0]
    pos = lax.broadcasted_iota(jnp.int32, (tt, 1), 0) + t * tt + 1

    for gi, w in enumerate(windows):
        @pl.when(grp == gi)
        def _(w=w):
            lo = V7X_SUBLANES
            s = ubuf[lo:n, :] + ubuf[lo - 1:n - 1, :]
            d = 2
            while d < w:
                sbuf[lo:n, :] = s
                s = sbuf[lo:n, :] + sbuf[lo - d:n - d, :]
                d *= 2
            wsum = s[POOL_HIST - lo:, :]
            cnt = jnp.minimum(pos, w).astype(F32)
            pbuf[...] = (wsum / cnt - u_ref[0]).astype(BF16)

    ubuf[0:POOL_HIST, :] = ubuf[tt:n, :]
    mixed = jnp.dot(pbuf[...], w_ref[0], preferred_element_type=F32) * sc_ref[...]
    y_ref[0] = (mixed * _silu(g_ref[0])).astype(BF16)


def _pool_mix(ug, w_grp, scale, B, T_pad):
    ngrp, gw, _ = w_grp.shape
    BR = ngrp * gw
    tt = _pick_tile(T_pad, 384, V7X_BF16_ROWS)
    ug3 = ug.reshape(B, T_pad, 2 * BR)
    body = functools.partial(_pool_body, windows=POOL_WINDOWS[:ngrp])
    blkbytes = 5 * _nbytes((tt, gw), F32) + _nbytes((gw, gw), BF16)
    return pl.pallas_call(
        body,
        grid=(ngrp, B, T_pad // tt),
        in_specs=[pl.BlockSpec((1, tt, gw), lambda g, b, t: (b, t, g)),
                  pl.BlockSpec((1, tt, gw), lambda g, b, t: (b, t, g + ngrp)),
                  pl.BlockSpec((1, gw, gw), lambda g, b, t: (g, 0, 0)),
                  pl.BlockSpec((1, gw), lambda g, b, t: (0, g))],
        out_specs=pl.BlockSpec((1, tt, gw), lambda g, b, t: (b, t, g)),
        out_shape=jax.ShapeDtypeStruct((B, T_pad, BR), BF16),
        scratch_shapes=[pltpu.VMEM((POOL_HIST + tt, gw), F32),
                        pltpu.VMEM((POOL_HIST + tt, gw), F32),
                        pltpu.VMEM((tt, gw), BF16)],
        compiler_params=_params(("parallel", "parallel", "arbitrary"), blkbytes),
        name="pool_mix",
    )(ug3, ug3, w_grp.astype(BF16), scale.reshape(1, BR))


def _rope_pairs(x, c_ref, s_ref):
    half = MLA_ROPE // 2
    lane = lax.broadcasted_iota(jnp.int32, x.shape, 1)
    partner = jnp.where(lane % MLA_ROPE < half,
                        pltpu.roll(x, V7X_LANES - half, 1), pltpu.roll(x, half, 1))
    return x * c_ref[...] + partner * s_ref[...]


def _rms_scale(x, g):
    return x * lax.rsqrt(jnp.mean(jnp.square(x), axis=-1, keepdims=True) + RMS_EPS) * g


def _qproj_body(c_ref, n_ref, w_ref, cos_ref, sin_ref, q_ref, xn_ref, *, ppb):
    @pl.when(pl.program_id(1) == 0)
    def _():
        xn_ref[...] = _rms_scale(c_ref[...], n_ref[...]).astype(BF16)

    res = jnp.dot(xn_ref[...], w_ref[...], preferred_element_type=F32)
    nn = 2 * MLA_NOPE
    pw = nn + 2 * MLA_ROPE
    for p in range(ppb):
        q_ref[:, p * pw:p * pw + nn] = res[:, p * pw:p * pw + nn].astype(BF16)
        q_ref[:, p * pw + nn:(p + 1) * pw] = _rope_pairs(
            res[:, p * pw + nn:(p + 1) * pw], cos_ref, sin_ref).astype(BF16)


def _kvproj_body(c_ref, kr_ref, n_ref, w_ref, cos_ref, sin_ref, kv_ref, krd_ref, xn_ref):
    @pl.when(pl.program_id(1) == 0)
    def _():
        xn_ref[...] = _rms_scale(c_ref[...], n_ref[...]).astype(BF16)
        rot = _rope_pairs(kr_ref[...], cos_ref, sin_ref)
        lane = lax.broadcasted_iota(jnp.int32, rot.shape, 1)
        krd_ref[...] = jnp.where(lane < MLA_ROPE, rot, pltpu.roll(rot, MLA_ROPE, 1)).astype(BF16)

    kv_ref[...] = jnp.dot(xn_ref[...], w_ref[...], preferred_element_type=F32).astype(BF16)


def _attn_body(q_ref, kn_ref, v_ref, kr_ref, g_ref, y_ref, m_ref, acc_ref, *, scale2, pps):
    qi = pl.program_id(2)
    tq = q_ref.shape[1]
    wide = 2 * tq
    nn = 2 * MLA_NOPE
    pw = nn + 2 * MLA_ROPE
    heads = [(pp, hh) for pp in range(pps) for hh in range(2)]
    lane = lax.broadcasted_iota(jnp.int32, (tq, V7X_LANES), 1)
    qs = []
    for pp, hh in heads:
        qr = q_ref[0, :, pp * pw + nn:(pp + 1) * pw]
        mine = (lane < MLA_ROPE) if hh == 0 else (lane >= MLA_ROPE)
        nope = q_ref[0, :, pp * pw + hh * MLA_NOPE:pp * pw + (hh + 1) * MLA_NOPE]
        qs.append(jnp.concatenate([nope, jnp.where(mine, qr, jnp.zeros_like(qr))], axis=1))

    m_ref[...] = jnp.full_like(m_ref, -jnp.inf)
    acc_ref[...] = jnp.zeros_like(acc_ref)

    def block(k0, tk, mask_shift):
        krope = kr_ref[0, pl.ds(k0, tk), :]
        ones = jnp.ones((tk, V7X_LANES), BF16)
        nc = tk // V7X_LANES
        for h, (pp, hh) in enumerate(heads):
            hc = slice((2 * pp + hh) * MLA_NOPE, (2 * pp + hh + 1) * MLA_NOPE)
            kf = jnp.concatenate([kn_ref[0, pl.ds(k0, tk), hc], krope], axis=1)
            s = lax.dot_general(qs[h], kf, (((1,), (1,)), ((), ())),
                                preferred_element_type=F32) * scale2
            if mask_shift is not None:
                rq = lax.broadcasted_iota(jnp.int32, (tq, tk), 0)
                ck = lax.broadcasted_iota(jnp.int32, (tq, tk), 1)
                s = jnp.where(ck <= rq + mask_shift, s, -1e30)
            cols = [s[:, c * V7X_LANES:(c + 1) * V7X_LANES] for c in range(nc)]
            mloc = cols[0]
            for c in range(1, nc):
                mloc = jnp.maximum(mloc, cols[c])
            m_old = m_ref[h]
            m_new = jnp.maximum(m_old, jnp.max(mloc, axis=1, keepdims=True))
            p = jnp.concatenate([jnp.exp2(cols[c] - m_new) for c in range(nc)], axis=1).astype(BF16)
            corr = jnp.exp2(m_old - m_new)
            vf = jnp.concatenate([v_ref[0, pl.ds(k0, tk), hc], ones], axis=1)
            pv = jnp.dot(p, vf, preferred_element_type=F32)
            acc_ref[h] = jnp.concatenate([corr, corr], axis=1) * acc_ref[h] + pv
            m_ref[h] = m_new

    def loop_body(j, carry):
        block(pl.multiple_of(j * wide, wide), wide, None)
        return carry

    lax.fori_loop(0, qi // 2, loop_body, 0)

    @pl.when(qi % 2 == 1)
    def _():
        block(pl.multiple_of((qi - 1) * tq, tq), wide, tq)

    @pl.when(qi % 2 == 0)
    def _():
        block(pl.multiple_of(qi * tq, tq), tq, 0)

    gate = _silu(g_ref[0])
    for h, (pp, hh) in enumerate(heads):
        hc = slice((2 * pp + hh) * MLA_NOPE, (2 * pp + hh + 1) * MLA_NOPE)
        acc = acc_ref[h]
        o = acc[:, :MLA_NOPE] / acc[:, MLA_NOPE:]
        y_ref[0, :, hc] = (o * gate[:, hc]).astype(BF16)


def _mla_mix(hb, w_in, q_norm, w_uq, kv_norm, w_ukv, B, T_pad):
    q_lora = q_norm.shape[0]
    kv_lora = kv_norm.shape[0]
    heads = w_uq.shape[1] // (MLA_NOPE + MLA_ROPE)
    vdim = w_ukv.shape[1] // heads - MLA_NOPE
    assert vdim == MLA_NOPE and heads % 2 == 0
    BR = heads * vdim
    npair = heads // 2
    M = hb.shape[0]
    D = hb.shape[1]

    n_in = w_in.shape[1]
    n_pad = -n_in % V7X_LANES
    w_in_p = jnp.pad(w_in, ((0, 0), (0, n_pad))).astype(BF16)
    c_all = _matmul(hb, w_in_p, F32, "mla_in", tn_target=1152)
    n_all = n_in + n_pad

    tm = _pick_tile(T_pad, 1056, V7X_BF16_ROWS)
    nt = T_pad // tm

    half = MLA_ROPE // 2
    inv = ROPE_BASE ** (-jnp.arange(0, MLA_ROPE, 2, dtype=F32) / MLA_ROPE)
    ang = jnp.arange(T_pad, dtype=F32)[:, None] * inv[None, :]
    cos_t = jnp.tile(jnp.cos(ang), (1, 4))
    sin_t = jnp.tile(jnp.concatenate([-jnp.sin(ang), jnp.sin(ang)], axis=1), (1, 2))
    del half

    hd = MLA_NOPE + MLA_ROPE
    wq = w_uq.reshape(q_lora, npair, 2, hd)
    wq = jnp.concatenate([wq[..., 0, :MLA_NOPE], wq[..., 1, :MLA_NOPE],
                          wq[..., 0, MLA_NOPE:], wq[..., 1, MLA_NOPE:]], axis=-1)
    pw = 2 * hd
    wq = wq.reshape(q_lora, npair * pw).astype(BF16)
    cq_blk = BR // q_lora
    ppb = _pick_tile(npair, 4, 1)
    qw = ppb * pw
    q = pl.pallas_call(
        functools.partial(_qproj_body, ppb=ppb),
        grid=(M // tm, npair // ppb),
        in_specs=[pl.BlockSpec((tm, q_lora), lambda i, j: (i, cq_blk)),
                  pl.BlockSpec((1, q_lora), lambda i, j: (0, 0)),
                  pl.BlockSpec((q_lora, qw), lambda i, j: (0, j)),
                  pl.BlockSpec((tm, V7X_LANES), lambda i, j: (i % nt, 0)),
                  pl.BlockSpec((tm, V7X_LANES), lambda i, j: (i % nt, 0))],
        out_specs=pl.BlockSpec((tm, qw), lambda i, j: (i, j)),
        out_shape=jax.ShapeDtypeStruct((M, npair * pw), BF16),
        scratch_shapes=[pltpu.VMEM((tm, q_lora), BF16)],
        compiler_params=_params(("parallel", "arbitrary"),
                                _nbytes((tm, q_lora), F32) + _nbytes((q_lora, qw), BF16)
                                + 2 * _nbytes((tm, qw), F32)),
        name="mla_qproj",
    )(c_all, q_norm.reshape(1, q_lora), wq, cos_t, sin_t)

    wkv = w_ukv.reshape(kv_lora, heads, MLA_NOPE + vdim)
    wkv = jnp.concatenate([wkv[..., :MLA_NOPE].reshape(kv_lora, BR),
                           wkv[..., MLA_NOPE:].reshape(kv_lora, BR)], axis=1).astype(BF16)
    tn = _pick_tile(2 * BR, 1024, V7X_LANES)
    ckv_blk = (BR + q_lora) // kv_lora
    kr_blk = (BR + q_lora + kv_lora) // V7X_LANES
    kv, krd = pl.pallas_call(
        _kvproj_body,
        grid=(M // tm, 2 * BR // tn),
        in_specs=[pl.BlockSpec((tm, kv_lora), lambda i, j: (i, ckv_blk)),
                  pl.BlockSpec((tm, V7X_LANES), lambda i, j: (i, kr_blk)),
                  pl.BlockSpec((1, kv_lora), lambda i, j: (0, 0)),
                  pl.BlockSpec((kv_lora, tn), lambda i, j: (0, j)),
                  pl.BlockSpec((tm, V7X_LANES), lambda i, j: (i % nt, 0)),
                  pl.BlockSpec((tm, V7X_LANES), lambda i, j: (i % nt, 0))],
        out_specs=[pl.BlockSpec((tm, tn), lambda i, j: (i, j)),
                   pl.BlockSpec((tm, V7X_LANES), lambda i, j: (i, 0))],
        out_shape=[jax.ShapeDtypeStruct((M, 2 * BR), BF16),
                   jax.ShapeDtypeStruct((M, V7X_LANES), BF16)],
        scratch_shapes=[pltpu.VMEM((tm, kv_lora), BF16)],
        compiler_params=_params(("parallel", "arbitrary"),
                                _nbytes((tm, kv_lora), F32) + _nbytes((kv_lora, tn), BF16)
                                + 2 * _nbytes((tm, tn), F32)),
        name="mla_kvproj",
    )(c_all, c_all, kv_norm.reshape(1, kv_lora), wkv, cos_t, sin_t)

    tq = _pick_tile(T_pad, 384, V7X_BF16_ROWS)
    pps = _pick_tile(npair, 4, 1)
    ngrp = npair // pps
    nh = 2 * pps
    pv = nh * vdim
    qw = pps * pw
    q3 = q.reshape(B, T_pad, npair * pw)
    kv3 = kv.reshape(B, T_pad, 2 * BR)
    krd3 = krd.reshape(B, T_pad, V7X_LANES)
    c3 = c_all.reshape(B, T_pad, n_all)
    body = functools.partial(_attn_body, scale2=float(hd) ** -0.5 * LOG2_E, pps=pps)
    blkbytes = (_nbytes((tq, qw), BF16) + 2 * _nbytes((T_pad, pv), BF16) + _nbytes((T_pad, V7X_LANES), BF16)
                + _nbytes((tq, pv), F32) + _nbytes((tq, pv), BF16) + 2 * nh * _nbytes((tq, 2 * tq), F32))
    return pl.pallas_call(
        body,
        grid=(B, ngrp, T_pad // tq),
        in_specs=[pl.BlockSpec((1, tq, qw), lambda b, p, i: (b, i, p)),
                  pl.BlockSpec((1, T_pad, pv), lambda b, p, i: (b, 0, p)),
                  pl.BlockSpec((1, T_pad, pv), lambda b, p, i: (b, 0, p + ngrp)),
                  pl.BlockSpec((1, T_pad, V7X_LANES), lambda b, p, i: (b, 0, 0)),
                  pl.BlockSpec((1, tq, pv), lambda b, p, i: (b, i, p))],
        out_specs=pl.BlockSpec((1, tq, pv), lambda b, p, i: (b, i, p)),
        out_shape=jax.ShapeDtypeStruct((B, T_pad, BR), BF16),
        scratch_shapes=[pltpu.VMEM((nh, tq, V7X_LANES), F32),
                        pltpu.VMEM((nh, tq, 2 * vdim), F32)],
        compiler_params=_params(("parallel", "parallel", "arbitrary"), blkbytes),
        name="mla_attn",
    )(q3, kv3, kv3, krd3, c3).reshape(M, BR)


def _mm_rope_body(a_ref, w_ref, cos_ref, sin_ref, o_ref, *, dk, scale):
    res = jnp.dot(a_ref[...], w_ref[...], preferred_element_type=F32)
    cos = cos_ref[...]
    sin = sin_ref[...]
    hd = dk // 2
    for h in range(o_ref.shape[1] // dk):
        x1 = res[:, h * dk:h * dk + hd]
        x2 = res[:, h * dk + hd:(h + 1) * dk]
        o_ref[:, h * dk:h * dk + hd] = ((x1 * cos - x2 * sin) * scale).astype(o_ref.dtype)
        o_ref[:, h * dk + hd:(h + 1) * dk] = ((x2 * cos + x1 * sin) * scale).astype(o_ref.dtype)


def _matmul_rope(a, w, cos_t, sin_t, dk, scale, out_dtype, name, tm_target=1056, tn_target=1024):
    M, K = a.shape
    N = w.shape[1]
    T_pad = cos_t.shape[0]
    tm = _pick_tile(T_pad, tm_target, V7X_BF16_ROWS)
    tn = _pick_tile(N, tn_target, dk)
    nt = T_pad // tm
    blk = (_nbytes((tm, K), a.dtype) + _nbytes((K, tn), w.dtype) + 2 * _nbytes((tm, tn), F32)
           + 2 * _nbytes((tm, dk // 2), F32))
    return pl.pallas_call(
        functools.partial(_mm_rope_body, dk=dk, scale=scale),
        grid=(M // tm, N // tn),
        in_specs=[pl.BlockSpec((tm, K), lambda i, j: (i, 0)),
                  pl.BlockSpec((K, tn), lambda i, j: (0, j)),
                  pl.BlockSpec((tm, dk // 2), lambda i, j: (i % nt, 0)),
                  pl.BlockSpec((tm, dk // 2), lambda i, j: (i % nt, 0))],
        out_specs=pl.BlockSpec((tm, tn), lambda i, j: (i, j)),
        out_shape=jax.ShapeDtypeStruct((M, N), out_dtype),
        compiler_params=_params(("parallel", "arbitrary"), blk),
        name=name,
    )(a, w, cos_t, sin_t)


def _ret_body(lg_ref, q_ref, k_ref, v_ref, g_ref, y_ref, st_ref, dec_ref, *, hps, dk):
    c = pl.program_id(2)
    C = q_ref.shape[1]

    @pl.when(c == 0)
    def _():
        st_ref[...] = jnp.zeros_like(st_ref)
        ri = lax.broadcasted_iota(jnp.int32, (C, C), 0)
        ci = lax.broadcasted_iota(jnp.int32, (C, C), 1)
        diff = (ri - ci).astype(F32)
        for h in range(hps):
            dec_ref[h] = jnp.where(diff >= 0, jnp.exp(jnp.maximum(diff, 0.0) * lg_ref[h, :, 0:1]), 0.0)

    jcol = lax.broadcasted_iota(jnp.int32, (C, 1), 0).astype(F32)
    for h in range(hps):
        cols = slice(h * dk, (h + 1) * dk)
        lg = lg_ref[h, :, 0:1]
        qb = q_ref[0, :, cols]
        kf = k_ref[0, :, cols]
        vb = v_ref[0, :, cols]
        s = lax.dot_general(qb, kf.astype(BF16), (((1,), (1,)), ((), ())), preferred_element_type=F32)
        intra = jnp.dot((s * dec_ref[h]).astype(BF16), vb, preferred_element_type=F32)
        state = st_ref[h]
        inter = jnp.dot(qb, state.astype(BF16), preferred_element_type=F32) * jnp.exp((jcol + 1.0) * lg)
        kw = (kf * jnp.exp((C - 1.0 - jcol) * lg)).astype(BF16)
        st_ref[h] = jnp.exp(C * lg) * state + lax.dot_general(
            kw, vb, (((0,), (0,)), ((), ())), preferred_element_type=F32)
        o = intra + inter
        mu = jnp.mean(o, axis=-1, keepdims=True)
        var = jnp.mean(jnp.square(o - mu), axis=-1, keepdims=True)
        o = (o - mu) * lax.rsqrt(var + LN_EPS)
        y_ref[0, :, cols] = (o * _silu(g_ref[0, :, cols])).astype(BF16)


def _ret_mix(hb, w_in, heads, B, T_pad):
    BR = w_in.shape[1] // 4
    dk = BR // heads
    C = _pick_tile(T_pad, 384, V7X_BF16_ROWS)
    hps = 2 if heads % 2 == 0 else 1
    inv = ROPE_BASE ** (-jnp.arange(0, dk, 2, dtype=F32) / dk)
    ang = jnp.arange(T_pad, dtype=F32)[:, None] * inv[None, :]
    cos_t = jnp.cos(ang)
    sin_t = jnp.sin(ang)
    wb = w_in.astype(BF16)
    q = _matmul_rope(hb, wb[:, :BR], cos_t, sin_t, dk, 1.0, BF16, "l3_q")
    k = _matmul_rope(hb, wb[:, BR:2 * BR], cos_t, sin_t, dk, float(dk) ** -0.5, F32, "l3_k")
    v = _matmul(hb, wb[:, 2 * BR:3 * BR], BF16, "l3_v")
    g = _matmul(hb, wb[:, 3 * BR:], F32, "l3_g")

    log_g = jnp.log(1.0 - 2.0 ** (-5.0 - jnp.arange(heads, dtype=F32)))
    lg = jnp.broadcast_to(log_g[:, None, None], (heads, 1, V7X_LANES))
    cw = hps * dk
    spec = pl.BlockSpec((1, C, cw), lambda b, h, c: (b, c, h))
    blkbytes = (2 * _nbytes((C, cw), F32) + 3 * _nbytes((C, cw), BF16)
                + hps * (4 * _nbytes((C, C), F32) + 2 * _nbytes((dk, dk), F32)))
    return pl.pallas_call(
        functools.partial(_ret_body, hps=hps, dk=dk),
        grid=(B, heads // hps, T_pad // C),
        in_specs=[pl.BlockSpec((hps, 1, V7X_LANES), lambda b, h, c: (h, 0, 0)), spec, spec, spec, spec],
        out_specs=spec,
        out_shape=jax.ShapeDtypeStruct((B, T_pad, BR), BF16),
        scratch_shapes=[pltpu.VMEM((hps, dk, dk), F32), pltpu.VMEM((hps, C, C), F32)],
        compiler_params=_params(("parallel", "parallel", "arbitrary"), blkbytes),
        name="ret_mix",
    )(lg, q.reshape(B, T_pad, BR), k.reshape(B, T_pad, BR), v.reshape(B, T_pad, BR),
      g.reshape(B, T_pad, BR)).reshape(B * T_pad, BR)


RET_HEADS = 16


def kernel(x, meta_tokens, l0_w_in, l0_conv_w, l0_conv_b, l0_w_a, l0_b_a, l0_w_x, l0_b_x, l0_lam, l0_w_out, l0_ln_g, l0_ln_b, l1_w_in, l1_w_grp, l1_scale, l1_w_out, l1_ln_g, l1_ln_b, l2_w_in, l2_q_norm, l2_w_uq, l2_kv_norm, l2_w_ukv, l2_w_out, l2_ln_g, l2_ln_b, l3_w_in, l3_w_out, l3_ln_g, l3_ln_b):
    B, S, D = x.shape
    n_meta = meta_tokens.shape[0]
    T = n_meta + S
    T_pad = -(-T // TIME_ALIGN) * TIME_ALIGN
    M = B * T_pad

    meta = jnp.broadcast_to(meta_tokens.astype(x.dtype)[None], (B, n_meta, D))
    h = jnp.concatenate([meta, x, jnp.zeros((B, T_pad - T, D), x.dtype)], axis=1).reshape(M, D)
    hb = h.astype(BF16)

    ug = _matmul(hb, l0_w_in.astype(BF16), F32, "l0_in")
    y = _rglru_mix(ug, l0_conv_w, l0_conv_b, l0_w_a, l0_b_a, l0_w_x, l0_b_x, l0_lam, B, T_pad)
    h, hb = _mm_res_ln(y.reshape(M, -1), l0_w_out.astype(BF16), h, l0_ln_g, l0_ln_b, "l0_out")

    ug = _matmul(hb, l1_w_in.astype(BF16), F32, "l1_in")
    y = _pool_mix(ug, l1_w_grp, l1_scale, B, T_pad)
    h, hb = _mm_res_ln(y.reshape(M, -1), l1_w_out.astype(BF16), h, l1_ln_g, l1_ln_b, "l1_out")

    y = _mla_mix(hb, l2_w_in, l2_q_norm, l2_w_uq, l2_kv_norm, l2_w_ukv, B, T_pad)
    h, hb = _mm_res_ln(y, l2_w_out.astype(BF16), h, l2_ln_g, l2_ln_b, "l2_out")

    y = _ret_mix(hb, l3_w_in, RET_HEADS, B, T_pad)
    h, hb = _mm_res_ln(y, l3_w_out.astype(BF16), h, l3_ln_g, l3_ln_b, "l3_out")

    return h.reshape(B, T_pad, D)[:, n_meta:T]
```

```python
import functools

import jax
import jax.numpy as jnp
from jax import lax
from jax.experimental import pallas as pl
from jax.experimental.pallas import tpu as pltpu

F32 = jnp.float32
BF16 = jnp.bfloat16

DEPTH = 4
ALPHA = (2.0 * DEPTH) ** 0.25
LN_EPS = 1e-5
RMS_EPS = 1e-6
ROPE_BASE = 10000.0
LRU_C = 8.0
POOL_WINDOWS = (2, 4, 8, 16)
MLA_NOPE = 128
MLA_ROPE = 64
LOG2_E = 1.4426950408889634

V7X_LANES = 128
V7X_SUBLANES = 8
V7X_BF16_ROWS = 16
V7X_SCOPED_VMEM_MAX = 60000 * 1024
V7X_VMEM_INTERNAL = 12 * 1024 * 1024

TIME_ALIGN = 128


def _pick_tile(n, target, mult):
    best = None
    for d in range(mult, min(n, target) + 1, mult):
        if n % d == 0:
            best = d
    return n if best is None else best


def _params(semantics, block_bytes):
    limit = min(2 * block_bytes + V7X_VMEM_INTERNAL, V7X_SCOPED_VMEM_MAX)
    return pltpu.CompilerParams(dimension_semantics=semantics, vmem_limit_bytes=int(limit))


def _nbytes(shape, dtype):
    n = 1
    for s in shape:
        n *= s
    return n * jnp.dtype(dtype).itemsize


def _sigmoid(x):
    return 1.0 / (1.0 + jnp.exp(-x))


def _silu(x):
    return x * _sigmoid(x)


def _mm_body(a_ref, w_ref, o_ref):
    o_ref[...] = jnp.dot(a_ref[...], w_ref[...], preferred_element_type=F32).astype(o_ref.dtype)


def _matmul(a, w, out_dtype, name, cols=None, tm_target=1056, tn_target=1024):
    M, K = a.shape
    start, N = (0, w.shape[1]) if cols is None else cols
    tm = _pick_tile(M, tm_target, V7X_BF16_ROWS)
    tn = _pick_tile(N, tn_target, V7X_LANES)
    assert start % tn == 0
    j0 = start // tn
    blk = _nbytes((tm, K), a.dtype) + _nbytes((K, tn), w.dtype) + _nbytes((tm, tn), out_dtype)
    return pl.pallas_call(
        _mm_body,
        grid=(M // tm, N // tn),
        in_specs=[pl.BlockSpec((tm, K), lambda i, j: (i, 0)),
                  pl.BlockSpec((K, tn), lambda i, j: (0, j + j0))],
        out_specs=pl.BlockSpec((tm, tn), lambda i, j: (i, j)),
        out_shape=jax.ShapeDtypeStruct((M, N), out_dtype),
        compiler_params=_params(("parallel", "arbitrary"), blk),
        name=name,
    )(a, w)


def _mm_res_body(y_ref, w_ref, h_ref, o_ref):
    o_ref[...] = ALPHA * h_ref[...] + jnp.dot(y_ref[...], w_ref[...], preferred_element_type=F32)


def _mm_lnres_body(y_ref, w_ref, z_ref, mu_ref, rs_ref, g_ref, b_ref, o_ref):
    h = (z_ref[...] - mu_ref[:, 0:1]) * rs_ref[:, 0:1] * g_ref[...] + b_ref[...]
    o_ref[...] = ALPHA * h + jnp.dot(y_ref[...], w_ref[...], preferred_element_type=F32)


def _out_proj(y, w, res, name, tm_target=1056, tn_target=512):
    M, K = y.shape
    N = w.shape[1]
    tm = _pick_tile(M, tm_target, V7X_BF16_ROWS)
    tn = _pick_tile(N, tn_target, V7X_LANES)
    blk = _nbytes((tm, K), BF16) + _nbytes((K, tn), BF16) + 3 * _nbytes((tm, tn), F32)
    tile = pl.BlockSpec((tm, tn), lambda i, j: (i, j))
    in_specs = [pl.BlockSpec((tm, K), lambda i, j: (i, 0)), pl.BlockSpec((K, tn), lambda i, j: (0, j)), tile]
    if len(res) == 1:
        body, args = _mm_res_body, (y, w, res[0])
    else:
        z, mu, rs, gamma, beta = res
        stat = pl.BlockSpec((tm, V7X_LANES), lambda i, j: (i, 0))
        vec = pl.BlockSpec((1, tn), lambda i, j: (0, j))
        in_specs += [stat, stat, vec, vec]
        body, args = _mm_lnres_body, (y, w, z, mu, rs, gamma.reshape(1, N), beta.reshape(1, N))
    return pl.pallas_call(
        body,
        grid=(M // tm, N // tn),
        in_specs=in_specs,
        out_specs=tile,
        out_shape=jax.ShapeDtypeStruct((M, N), F32),
        compiler_params=_params(("parallel", "arbitrary"), blk),
        name=name,
    )(*args)


LN_ROWS = 16
LN_UNROLL = 4


def _row_sum(x):
    w = x.shape[1]
    while w > V7X_LANES and w % (2 * V7X_LANES) == 0:
        w //= 2
        x = x[:, :w] + x[:, w:]
    return jnp.sum(x, axis=1, keepdims=True)


def _ln_rows(r):
    return pl.ds(pl.multiple_of(r * LN_ROWS, LN_ROWS), LN_ROWS)


def _ln_stats(load, n, mu_ref, rs_ref, d_model):
    inv_d = 1.0 / d_model
    stat = (LN_ROWS, V7X_LANES)

    def mean_pass(r, carry):
        mu_ref[_ln_rows(r), :] = jnp.broadcast_to(_row_sum(load(r)) * inv_d, stat)
        return carry

    def var_pass(r, carry):
        d = load(r) - mu_ref[_ln_rows(r), 0:1]
        rs_ref[_ln_rows(r), :] = jnp.broadcast_to(lax.rsqrt(_row_sum(d * d) * inv_d + LN_EPS), stat)
        return carry

    lax.fori_loop(0, n, mean_pass, 0, unroll=LN_UNROLL)
    lax.fori_loop(0, n, var_pass, 0, unroll=LN_UNROLL)


def _ln_apply(zr, r, mu_ref, rs_ref, g_ref, b_ref):
    return (zr - mu_ref[_ln_rows(r), 0:1]) * rs_ref[_ln_rows(r), 0:1] * g_ref[...] + b_ref[...]


def _ln_body(z_ref, g_ref, b_ref, ohb_ref, mu_ref, rs_ref):
    tr, D = z_ref.shape
    n = tr // LN_ROWS

    def load(r):
        return z_ref[_ln_rows(r), :]

    _ln_stats(load, n, mu_ref, rs_ref, D)

    def norm_pass(r, carry):
        ohb_ref[_ln_rows(r), :] = _ln_apply(load(r), r, mu_ref, rs_ref, g_ref, b_ref).astype(BF16)
        return carry

    lax.fori_loop(0, n, norm_pass, 0, unroll=2)


def _layer_norm(z, gamma, beta, name, tr_target=256):
    M, D = z.shape
    tr = _pick_tile(M, tr_target, LN_ROWS * LN_UNROLL)
    blk = _nbytes((tr, D), F32) + _nbytes((tr, D), BF16)
    stat = pl.BlockSpec((tr, V7X_LANES), lambda i: (i, 0))
    return pl.pallas_call(
        _ln_body,
        grid=(M // tr,),
        in_specs=[pl.BlockSpec((tr, D), lambda i: (i, 0)),
                  pl.BlockSpec((1, D), lambda i: (0, 0)),
                  pl.BlockSpec((1, D), lambda i: (0, 0))],
        out_specs=[pl.BlockSpec((tr, D), lambda i: (i, 0)), stat, stat],
        out_shape=[jax.ShapeDtypeStruct((M, D), BF16),
                   jax.ShapeDtypeStruct((M, V7X_LANES), F32),
                   jax.ShapeDtypeStruct((M, V7X_LANES), F32)],
        compiler_params=_params(("parallel",), blk),
        name=name,
    )(z, gamma.reshape(1, D), beta.reshape(1, D))


def _ln_final_body(za_ref, zb_ref, g_ref, b_ref, o_ref, mu_ref, rs_ref, *, shift):
    tr, D = o_ref.shape[1], o_ref.shape[2]
    n = tr // LN_ROWS

    def load(r):
        return za_ref[0, _ln_rows(r), :]

    _ln_stats(load, n, mu_ref, rs_ref, D)

    def norm_pass(r, carry):
        o_ref[0, _ln_rows(r - shift), :] = _ln_apply(load(r), r, mu_ref, rs_ref, g_ref, b_ref)
        return carry

    lax.fori_loop(shift, n, norm_pass, 0)
    for c in range(shift):
        zr = zb_ref[0, c * LN_ROWS:(c + 1) * LN_ROWS, :]
        mu = _row_sum(zr) * (1.0 / D)
        d = zr - mu
        rs = lax.rsqrt(_row_sum(d * d) * (1.0 / D) + LN_EPS)
        o_ref[0, (n - shift + c) * LN_ROWS:(n - shift + c + 1) * LN_ROWS, :] = d * rs * g_ref[...] + b_ref[...]


def _layer_norm_final(z, gamma, beta, B, T_pad, n_meta, S, name, tr_target=256):
    D = z.shape[1]
    assert n_meta % LN_ROWS == 0
    tr = _pick_tile(S, tr_target, LN_ROWS * LN_UNROLL)
    assert tr % n_meta == 0
    per = tr // n_meta
    blk = 2 * _nbytes((tr, D), F32) + _nbytes((n_meta, D), F32)
    z3 = z.reshape(B, T_pad, D)
    return pl.pallas_call(
        functools.partial(_ln_final_body, shift=n_meta // LN_ROWS),
        grid=(B, S // tr),
        in_specs=[pl.BlockSpec((1, tr, D), lambda b, t: (b, t, 0)),
                  pl.BlockSpec((1, n_meta, D), lambda b, t: (b, (t + 1) * per, 0)),
                  pl.BlockSpec((1, D), lambda b, t: (0, 0)),
                  pl.BlockSpec((1, D), lambda b, t: (0, 0))],
        out_specs=pl.BlockSpec((1, tr, D), lambda b, t: (b, t, 0)),
        out_shape=jax.ShapeDtypeStruct((B, S, D), F32),
        scratch_shapes=[pltpu.VMEM((tr, V7X_LANES), F32), pltpu.VMEM((tr, V7X_LANES), F32)],
        compiler_params=_params(("parallel", "parallel"), blk),
        name=name,
    )(z3, z3, gamma.reshape(1, D), beta.reshape(1, D))


def _shift_rows(v, d, fill, row):
    n = v.shape[0]
    if d % V7X_SUBLANES == 0:
        return jnp.concatenate([jnp.full((d,) + v.shape[1:], fill, v.dtype), v[:n - d]], axis=0)
    return jnp.where(row >= d, pltpu.roll(v, d, 0), fill)


CONV_HIST = 8
SCAN_ROWS = 32


def _rglru_body(u_ref, g_ref, cw_ref, cb_ref, wg_ref, bg_ref, lam_ref, y_ref,
                ubuf, hcar, sa, sx, *, conv_w, hpb, blk):
    t = pl.program_id(2)
    tt = u_ref.shape[1]
    cw = u_ref.shape[2]

    @pl.when(t == 0)
    def _():
        ubuf[0:CONV_HIST, :] = jnp.zeros((CONV_HIST, cw), F32)
        hcar[...] = jnp.zeros_like(hcar)

    ubuf[CONV_HIST:CONV_HIST + tt, :] = u_ref[0]
    uc = cb_ref[...] + cw_ref[conv_w - 1:conv_w, :] * ubuf[CONV_HIST:CONV_HIST + tt, :]
    for k in range(conv_w - 1):
        off = CONV_HIST - (conv_w - 1) + k
        uc = uc + cw_ref[k:k + 1, :] * ubuf[off:off + tt, :]
    ubuf[0:CONV_HIST, :] = ubuf[tt:tt + CONV_HIST, :]

    lam = lam_ref[...]
    logsig = -(jnp.maximum(-lam, 0.0) + jnp.log1p(jnp.exp(-jnp.abs(lam))))
    for hh in range(hpb):
        cols = slice(hh * blk, (hh + 1) * blk)
        ucs = uc[:, cols]
        gate = jnp.dot(ucs.astype(BF16), wg_ref[hh], preferred_element_type=F32) + bg_ref[hh]
        r = _sigmoid(gate[:, :blk])
        i = _sigmoid(gate[:, blk:])
        a = jnp.exp(LRU_C * r * logsig[:, cols])
        sa[:, cols] = a
        sx[:, cols] = (ucs * i) * jnp.sqrt(1.0 - a * a)

    row = lax.broadcasted_iota(jnp.int32, (SCAN_ROWS, cw), 0)
    h_in = hcar[...]
    for s in range(tt // SCAN_ROWS):
        rows = slice(s * SCAN_ROWS, (s + 1) * SCAN_ROWS)
        a = sa[rows, :]
        x = sx[rows, :]
        d = 1
        while d < SCAN_ROWS:
            x = a * _shift_rows(x, d, 0.0, row) + x
            a = a * _shift_rows(a, d, 1.0, row)
            d *= 2
        h = x + a * h_in
        h_in = h[SCAN_ROWS - 1:SCAN_ROWS, :]
        y_ref[0, rows, :] = (h * _silu(g_ref[0, rows, :])).astype(BF16)
    hcar[...] = h_in


def _rglru_mix(ug, conv_w, conv_b, w_a, b_a, w_x, b_x, lam, B, T_pad):
    BR = conv_b.shape[0]
    heads, blk, _ = w_a.shape
    K = conv_w.shape[0]
    hpb = 2 if heads % 2 == 0 else 1
    cw = hpb * blk
    ncb = BR // cw
    tt = _pick_tile(T_pad, 384, SCAN_ROWS)
    wg = jnp.concatenate([w_a, w_x], axis=-1).astype(BF16)
    bg = jnp.concatenate([b_a.reshape(heads, 1, blk), b_x.reshape(heads, 1, blk)], axis=-1)
    ug3 = ug.reshape(B, T_pad, 2 * BR)
    body = functools.partial(_rglru_body, conv_w=K, hpb=hpb, blk=blk)
    blkbytes = 3 * _nbytes((tt, cw), F32) + _nbytes((hpb, blk, 2 * blk), BF16) + 2 * _nbytes((tt, cw), F32)
    return pl.pallas_call(
        body,
        grid=(B, ncb, T_pad // tt),
        in_specs=[pl.BlockSpec((1, tt, cw), lambda b, c, t: (b, t, c)),
                  pl.BlockSpec((1, tt, cw), lambda b, c, t: (b, t, c + ncb)),
                  pl.BlockSpec((K, cw), lambda b, c, t: (0, c)),
                  pl.BlockSpec((1, cw), lambda b, c, t: (0, c)),
                  pl.BlockSpec((hpb, blk, 2 * blk), lambda b, c, t: (c, 0, 0)),
                  pl.BlockSpec((hpb, 1, 2 * blk), lambda b, c, t: (c, 0, 0)),
                  pl.BlockSpec((1, cw), lambda b, c, t: (0, c))],
        out_specs=pl.BlockSpec((1, tt, cw), lambda b, c, t: (b, t, c)),
        out_shape=jax.ShapeDtypeStruct((B, T_pad, BR), BF16),
        scratch_shapes=[pltpu.VMEM((CONV_HIST + tt, cw), F32),
                        pltpu.VMEM((1, cw), F32),
                        pltpu.VMEM((tt, cw), F32),
                        pltpu.VMEM((tt, cw), F32)],
        compiler_params=_params(("parallel", "parallel", "arbitrary"), blkbytes),
        name="rglru_mix",
    )(ug3, ug3, conv_w.reshape(K, BR), conv_b.reshape(1, BR), wg, bg, lam.reshape(1, BR))


POOL_HIST = 24


def _pool_body(u_ref, g_ref, w_ref, sc_ref, y_ref, ubuf, sbuf, pbuf, *, windows):
    grp = pl.program_id(0)
    t = pl.program_id(2)
    tt = u_ref.shape[1]
    gw = u_ref.shape[2]
    n = POOL_HIST + tt

    @pl.when(t == 0)
    def _():
        ubuf[0:POOL_HIST, :] = jnp.zeros((POOL_HIST, gw), F32)

    sbuf[0:V7X_SUBLANES, :] = jnp.zeros((V7X_SUBLANES, gw), F32)
    ubuf[POOL_HIST:n, :] = u_ref[0]
    pos = lax.broadcasted_iota(jnp.int32, (tt, 1), 0) + t * tt + 1

    for gi, w in enumerate(windows):
        @pl.when(grp == gi)
        def _(w=w):
            lo = V7X_SUBLANES
            s = ubuf[lo:n, :] + ubuf[lo - 1:n - 1, :]
            d = 2
            while d < w:
                sbuf[lo:n, :] = s
                s = sbuf[lo:n, :] + sbuf[lo - d:n - d, :]
                d *= 2
            wsum = s[POOL_HIST - lo:, :]
            cnt = jnp.minimum(pos, w).astype(F32)
            pbuf[...] = (wsum / cnt - u_ref[0]).astype(BF16)

    ubuf[0:POOL_HIST, :] = ubuf[tt:n, :]
    mixed = jnp.dot(pbuf[...], w_ref[0], preferred_element_type=F32) * sc_ref[...]
    y_ref[0] = (mixed * _silu(g_ref[0])).astype(BF16)


def _pool_mix(ug, w_grp, scale, B, T_pad):
    ngrp, gw, _ = w_grp.shape
    BR = ngrp * gw
    tt = _pick_tile(T_pad, 384, V7X_BF16_ROWS)
    ug3 = ug.reshape(B, T_pad, 2 * BR)
    body = functools.partial(_pool_body, windows=POOL_WINDOWS[:ngrp])
    blkbytes = 5 * _nbytes((tt, gw), F32) + _nbytes((gw, gw), BF16)
    return pl.pallas_call(
        body,
        grid=(ngrp, B, T_pad // tt),
        in_specs=[pl.BlockSpec((1, tt, gw), lambda g, b, t: (b, t, g)),
                  pl.BlockSpec((1, tt, gw), lambda g, b, t: (b, t, g + ngrp)),
                  pl.BlockSpec((1, gw, gw), lambda g, b, t: (g, 0, 0)),
                  pl.BlockSpec((1, gw), lambda g, b, t: (0, g))],
        out_specs=pl.BlockSpec((1, tt, gw), lambda g, b, t: (b, t, g)),
        out_shape=jax.ShapeDtypeStruct((B, T_pad, BR), BF16),
        scratch_shapes=[pltpu.VMEM((POOL_HIST + tt, gw), F32),
                        pltpu.VMEM((POOL_HIST + tt, gw), F32),
                        pltpu.VMEM((tt, gw), BF16)],
        compiler_params=_params(("parallel", "parallel", "arbitrary"), blkbytes),
        name="pool_mix",
    )(ug3, ug3, w_grp.astype(BF16), scale.reshape(1, BR))


def _rope_pairs(x, c_ref, s_ref):
    half = MLA_ROPE // 2
    lane = lax.broadcasted_iota(jnp.int32, x.shape, 1)
    partner = jnp.where(lane % MLA_ROPE < half,
                        pltpu.roll(x, V7X_LANES - half, 1), pltpu.roll(x, half, 1))
    return x * c_ref[...] + partner * s_ref[...]


def _rms_scale(x, g):
    return x * lax.rsqrt(jnp.mean(jnp.square(x), axis=-1, keepdims=True) + RMS_EPS) * g


def _qproj_body(c_ref, n_ref, w_ref, cos_ref, sin_ref, q_ref, xn_ref, *, ppb):
    @pl.when(pl.program_id(1) == 0)
    def _():
        xn_ref[...] = _rms_scale(c_ref[...], n_ref[...]).astype(BF16)

    res = jnp.dot(xn_ref[...], w_ref[...], preferred_element_type=F32)
    nn = 2 * MLA_NOPE
    pw = nn + 2 * MLA_ROPE
    for p in range(ppb):
        q_ref[:, p * pw:p * pw + nn] = res[:, p * pw:p * pw + nn].astype(BF16)
        q_ref[:, p * pw + nn:(p + 1) * pw] = _rope_pairs(
            res[:, p * pw + nn:(p + 1) * pw], cos_ref, sin_ref).astype(BF16)


def _kvproj_body(c_ref, kr_ref, n_ref, w_ref, cos_ref, sin_ref, kv_ref, krd_ref, xn_ref):
    @pl.when(pl.program_id(1) == 0)
    def _():
        xn_ref[...] = _rms_scale(c_ref[...], n_ref[...]).astype(BF16)
        rot = _rope_pairs(kr_ref[...], cos_ref, sin_ref)
        lane = lax.broadcasted_iota(jnp.int32, rot.shape, 1)
        krd_ref[...] = jnp.where(lane < MLA_ROPE, rot, pltpu.roll(rot, MLA_ROPE, 1)).astype(BF16)

    kv_ref[...] = jnp.dot(xn_ref[...], w_ref[...], preferred_element_type=F32).astype(BF16)


def _attn_body(q_ref, kn_ref, v_ref, kr_ref, g_ref, y_ref, m_ref, acc_ref, *, scale2, pps):
    qi = pl.program_id(2)
    tq = q_ref.shape[1]
    wide = 2 * tq
    nn = 2 * MLA_NOPE
    pw = nn + 2 * MLA_ROPE
    heads = [(pp, hh) for pp in range(pps) for hh in range(2)]
    lane = lax.broadcasted_iota(jnp.int32, (tq, V7X_LANES), 1)
    qs = []
    for pp, hh in heads:
        qr = q_ref[0, :, pp * pw + nn:(pp + 1) * pw]
        mine = (lane < MLA_ROPE) if hh == 0 else (lane >= MLA_ROPE)
        nope = q_ref[0, :, pp * pw + hh * MLA_NOPE:pp * pw + (hh + 1) * MLA_NOPE]
        qs.append(jnp.concatenate([nope, jnp.where(mine, qr, jnp.zeros_like(qr))], axis=1))

    m_ref[...] = jnp.full_like(m_ref, -jnp.inf)
    acc_ref[...] = jnp.zeros_like(acc_ref)

    def block(k0, tk, mask_shift):
        krope = kr_ref[0, pl.ds(k0, tk), :]
        ones = jnp.ones((tk, V7X_LANES), BF16)
        nc = tk // V7X_LANES
        for h, (pp, hh) in enumerate(heads):
            hc = slice((2 * pp + hh) * MLA_NOPE, (2 * pp + hh + 1) * MLA_NOPE)
            kf = jnp.concatenate([kn_ref[0, pl.ds(k0, tk), hc], krope], axis=1)
            s = lax.dot_general(qs[h], kf, (((1,), (1,)), ((), ())),
                                preferred_element_type=F32) * scale2
            if mask_shift is not None:
                rq = lax.broadcasted_iota(jnp.int32, (tq, tk), 0)
                ck = lax.broadcasted_iota(jnp.int32, (tq, tk), 1)
                s = jnp.where(ck <= rq + mask_shift, s, -1e30)
            cols = [s[:, c * V7X_LANES:(c + 1) * V7X_LANES] for c in range(nc)]
            mloc = cols[0]
            for c in range(1, nc):
                mloc = jnp.maximum(mloc, cols[c])
            m_old = m_ref[h]
            m_new = jnp.maximum(m_old, jnp.max(mloc, axis=1, keepdims=True))
            p = jnp.concatenate([jnp.exp2(cols[c] - m_new) for c in range(nc)], axis=1).astype(BF16)
            corr = jnp.exp2(m_old - m_new)
            vf = jnp.concatenate([v_ref[0, pl.ds(k0, tk), hc], ones], axis=1)
            pv = jnp.dot(p, vf, preferred_element_type=F32)
            acc_ref[h] = jnp.concatenate([corr, corr], axis=1) * acc_ref[h] + pv
            m_ref[h] = m_new

    def loop_body(j, carry):
        block(pl.multiple_of(j * wide, wide), wide, None)
        return carry

    lax.fori_loop(0, qi // 2, loop_body, 0)

    @pl.when(qi % 2 == 1)
    def _():
        block(pl.multiple_of((qi - 1) * tq, tq), wide, tq)

    @pl.when(qi % 2 == 0)
    def _():
        block(pl.multiple_of(qi * tq, tq), tq, 0)

    gate = _silu(g_ref[0])
    for h, (pp, hh) in enumerate(heads):
        hc = slice((2 * pp + hh) * MLA_NOPE, (2 * pp + hh + 1) * MLA_NOPE)
        acc = acc_ref[h]
        o = acc[:, :MLA_NOPE] / acc[:, MLA_NOPE:]
        y_ref[0, :, hc] = (o * gate[:, hc]).astype(BF16)


def _mla_mix(hb, w_in, q_norm, w_uq, kv_norm, w_ukv, B, T_pad):
    q_lora = q_norm.shape[0]
    kv_lora = kv_norm.shape[0]
    heads = w_uq.shape[1] // (MLA_NOPE + MLA_ROPE)
    vdim = w_ukv.shape[1] // heads - MLA_NOPE
    assert vdim == MLA_NOPE and heads % 2 == 0
    BR = heads * vdim
    npair = heads // 2
    M = hb.shape[0]
    D = hb.shape[1]

    n_in = w_in.shape[1]
    n_pad = -n_in % V7X_LANES
    w_in_p = jnp.pad(w_in, ((0, 0), (0, n_pad))).astype(BF16)
    c_all = _matmul(hb, w_in_p, F32, "mla_in", tn_target=1152)
    n_all = n_in + n_pad

    tm = _pick_tile(T_pad, 1056, V7X_BF16_ROWS)
    nt = T_pad // tm

    half = MLA_ROPE // 2
    inv = ROPE_BASE ** (-jnp.arange(0, MLA_ROPE, 2, dtype=F32) / MLA_ROPE)
    ang = jnp.arange(T_pad, dtype=F32)[:, None] * inv[None, :]
    cos_t = jnp.tile(jnp.cos(ang), (1, 4))
    sin_t = jnp.tile(jnp.concatenate([-jnp.sin(ang), jnp.sin(ang)], axis=1), (1, 2))
    del half

    hd = MLA_NOPE + MLA_ROPE
    wq = w_uq.reshape(q_lora, npair, 2, hd)
    wq = jnp.concatenate([wq[..., 0, :MLA_NOPE], wq[..., 1, :MLA_NOPE],
                          wq[..., 0, MLA_NOPE:], wq[..., 1, MLA_NOPE:]], axis=-1)
    pw = 2 * hd
    wq = wq.reshape(q_lora, npair * pw).astype(BF16)
    cq_blk = BR // q_lora
    ppb = _pick_tile(npair, 4, 1)
    qw = ppb * pw
    q = pl.pallas_call(
        functools.partial(_qproj_body, ppb=ppb),
        grid=(M // tm, npair // ppb),
        in_specs=[pl.BlockSpec((tm, q_lora), lambda i, j: (i, cq_blk)),
                  pl.BlockSpec((1, q_lora), lambda i, j: (0, 0)),
                  pl.BlockSpec((q_lora, qw), lambda i, j: (0, j)),
                  pl.BlockSpec((tm, V7X_LANES), lambda i, j: (i % nt, 0)),
                  pl.BlockSpec((tm, V7X_LANES), lambda i, j: (i % nt, 0))],
        out_specs=pl.BlockSpec((tm, qw), lambda i, j: (i, j)),
        out_shape=jax.ShapeDtypeStruct((M, npair * pw), BF16),
        scratch_shapes=[pltpu.VMEM((tm, q_lora), BF16)],
        compiler_params=_params(("parallel", "arbitrary"),
                                _nbytes((tm, q_lora), F32) + _nbytes((q_lora, qw), BF16)
                                + 2 * _nbytes((tm, qw), F32)),
        name="mla_qproj",
    )(c_all, q_norm.reshape(1, q_lora), wq, cos_t, sin_t)

    wkv = w_ukv.reshape(kv_lora, heads, MLA_NOPE + vdim)
    wkv = jnp.concatenate([wkv[..., :MLA_NOPE].reshape(kv_lora, BR),
                           wkv[..., MLA_NOPE:].reshape(kv_lora, BR)], axis=1).astype(BF16)
    tn = _pick_tile(2 * BR, 1024, V7X_LANES)
    ckv_blk = (BR + q_lora) // kv_lora
    kr_blk = (BR + q_lora + kv_lora) // V7X_LANES
    kv, krd = pl.pallas_call(
        _kvproj_body,
        grid=(M // tm, 2 * BR // tn),
        in_specs=[pl.BlockSpec((tm, kv_lora), lambda i, j: (i, ckv_blk)),
                  pl.BlockSpec((tm, V7X_LANES), lambda i, j: (i, kr_blk)),
                  pl.BlockSpec((1, kv_lora), lambda i, j: (0, 0)),
                  pl.BlockSpec((kv_lora, tn), lambda i, j: (0, j)),
                  pl.BlockSpec((tm, V7X_LANES), lambda i, j: (i % nt, 0)),
                  pl.BlockSpec((tm, V7X_LANES), lambda i, j: (i % nt, 0))],
        out_specs=[pl.BlockSpec((tm, tn), lambda i, j: (i, j)),
                   pl.BlockSpec((tm, V7X_LANES), lambda i, j: (i, 0))],
        out_shape=[jax.ShapeDtypeStruct((M, 2 * BR), BF16),
                   jax.ShapeDtypeStruct((M, V7X_LANES), BF16)],
        scratch_shapes=[pltpu.VMEM((tm, kv_lora), BF16)],
        compiler_params=_params(("parallel", "arbitrary"),
                                _nbytes((tm, kv_lora), F32) + _nbytes((kv_lora, tn), BF16)
                                + 2 * _nbytes((tm, tn), F32)),
        name="mla_kvproj",
    )(c_all, c_all, kv_norm.reshape(1, kv_lora), wkv, cos_t, sin_t)

    tq = _pick_tile(T_pad, 384, V7X_BF16_ROWS)
    pps = _pick_tile(npair, 4, 1)
    ngrp = npair // pps
    nh = 2 * pps
    pv = nh * vdim
    qw = pps * pw
    q3 = q.reshape(B, T_pad, npair * pw)
    kv3 = kv.reshape(B, T_pad, 2 * BR)
    krd3 = krd.reshape(B, T_pad, V7X_LANES)
    c3 = c_all.reshape(B, T_pad, n_all)
    body = functools.partial(_attn_body, scale2=float(hd) ** -0.5 * LOG2_E, pps=pps)
    blkbytes = (_nbytes((tq, qw), BF16) + 2 * _nbytes((T_pad, pv), BF16) + _nbytes((T_pad, V7X_LANES), BF16)
                + _nbytes((tq, pv), F32) + _nbytes((tq, pv), BF16) + 2 * nh * _nbytes((tq, 2 * tq), F32))
    return pl.pallas_call(
        body,
        grid=(B, ngrp, T_pad // tq),
        in_specs=[pl.BlockSpec((1, tq, qw), lambda b, p, i: (b, i, p)),
                  pl.BlockSpec((1, T_pad, pv), lambda b, p, i: (b, 0, p)),
                  pl.BlockSpec((1, T_pad, pv), lambda b, p, i: (b, 0, p + ngrp)),
                  pl.BlockSpec((1, T_pad, V7X_LANES), lambda b, p, i: (b, 0, 0)),
                  pl.BlockSpec((1, tq, pv), lambda b, p, i: (b, i, p))],
        out_specs=pl.BlockSpec((1, tq, pv), lambda b, p, i: (b, i, p)),
        out_shape=jax.ShapeDtypeStruct((B, T_pad, BR), BF16),
        scratch_shapes=[pltpu.VMEM((nh, tq, V7X_LANES), F32),
                        pltpu.VMEM((nh, tq, 2 * vdim), F32)],
        compiler_params=_params(("parallel", "parallel", "arbitrary"), blkbytes),
        name="mla_attn",
    )(q3, kv3, kv3, krd3, c3).reshape(M, BR)


def _mm_rope_body(a_ref, w_ref, cos_ref, sin_ref, o_ref, *, dk, scale):
    res = jnp.dot(a_ref[...], w_ref[...], preferred_element_type=F32)
    cos = cos_ref[...]
    sin = sin_ref[...]
    hd = dk // 2
    for h in range(o_ref.shape[1] // dk):
        x1 = res[:, h * dk:h * dk + hd]
        x2 = res[:, h * dk + hd:(h + 1) * dk]
        o_ref[:, h * dk:h * dk + hd] = ((x1 * cos - x2 * sin) * scale).astype(o_ref.dtype)
        o_ref[:, h * dk + hd:(h + 1) * dk] = ((x2 * cos + x1 * sin) * scale).astype(o_ref.dtype)


def _matmul_rope(a, w, cols, cos_t, sin_t, dk, scale, out_dtype, name, tm_target=1056, tn_target=1024):
    M, K = a.shape
    start, N = cols
    T_pad = cos_t.shape[0]
    tm = _pick_tile(T_pad, tm_target, V7X_BF16_ROWS)
    tn = _pick_tile(N, tn_target, dk)
    assert start % tn == 0
    j0 = start // tn
    nt = T_pad // tm
    blk = (_nbytes((tm, K), a.dtype) + _nbytes((K, tn), w.dtype) + 2 * _nbytes((tm, tn), F32)
           + 2 * _nbytes((tm, dk // 2), F32))
    return pl.pallas_call(
        functools.partial(_mm_rope_body, dk=dk, scale=scale),
        grid=(M // tm, N // tn),
        in_specs=[pl.BlockSpec((tm, K), lambda i, j: (i, 0)),
                  pl.BlockSpec((K, tn), lambda i, j: (0, j + j0)),
                  pl.BlockSpec((tm, dk // 2), lambda i, j: (i % nt, 0)),
                  pl.BlockSpec((tm, dk // 2), lambda i, j: (i % nt, 0))],
        out_specs=pl.BlockSpec((tm, tn), lambda i, j: (i, j)),
        out_shape=jax.ShapeDtypeStruct((M, N), out_dtype),
        compiler_params=_params(("parallel", "arbitrary"), blk),
        name=name,
    )(a, w, cos_t, sin_t)


def _ret_body(lg_ref, q_ref, k_ref, v_ref, g_ref, y_ref, st_ref, dec_ref, *, hps, dk):
    c = pl.program_id(2)
    C = q_ref.shape[1]

    @pl.when(c == 0)
    def _():
        st_ref[...] = jnp.zeros_like(st_ref)
        ri = lax.broadcasted_iota(jnp.int32, (C, C), 0)
        ci = lax.broadcasted_iota(jnp.int32, (C, C), 1)
        diff = (ri - ci).astype(F32)
        for h in range(hps):
            dec_ref[h] = jnp.where(diff >= 0, jnp.exp(jnp.maximum(diff, 0.0) * lg_ref[h, :, 0:1]), 0.0)

    jcol = lax.broadcasted_iota(jnp.int32, (C, 1), 0).astype(F32)
    for h in range(hps):
        cols = slice(h * dk, (h + 1) * dk)
        lg = lg_ref[h, :, 0:1]
        qb = q_ref[0, :, cols]
        kf = k_ref[0, :, cols]
        vb = v_ref[0, :, cols]
        s = lax.dot_general(qb, kf.astype(BF16), (((1,), (1,)), ((), ())), preferred_element_type=F32)
        intra = jnp.dot((s * dec_ref[h]).astype(BF16), vb, preferred_element_type=F32)
        state = st_ref[h]
        inter = jnp.dot(qb, state.astype(BF16), preferred_element_type=F32) * jnp.exp((jcol + 1.0) * lg)
        kw = (kf * jnp.exp((C - 1.0 - jcol) * lg)).astype(BF16)
        st_ref[h] = jnp.exp(C * lg) * state + lax.dot_general(
            kw, vb, (((0,), (0,)), ((), ())), preferred_element_type=F32)
        o = intra + inter
        mu = jnp.mean(o, axis=-1, keepdims=True)
        var = jnp.mean(jnp.square(o - mu), axis=-1, keepdims=True)
        o = (o - mu) * lax.rsqrt(var + LN_EPS)
        y_ref[0, :, cols] = (o * _silu(g_ref[0, :, cols])).astype(BF16)


def _ret_mix(hb, w_in, heads, B, T_pad):
    BR = w_in.shape[1] // 4
    dk = BR // heads
    C = _pick_tile(T_pad, 384, V7X_BF16_ROWS)
    hps = 2 if heads % 2 == 0 else 1
    inv = ROPE_BASE ** (-jnp.arange(0, dk, 2, dtype=F32) / dk)
    ang = jnp.arange(T_pad, dtype=F32)[:, None] * inv[None, :]
    cos_t = jnp.cos(ang)
    sin_t = jnp.sin(ang)
    wb = w_in.astype(BF16)
    q = _matmul_rope(hb, wb, (0, BR), cos_t, sin_t, dk, 1.0, BF16, "l3_q")
    k = _matmul_rope(hb, wb, (BR, BR), cos_t, sin_t, dk, float(dk) ** -0.5, F32, "l3_k")
    v = _matmul(hb, wb, BF16, "l3_v", cols=(2 * BR, BR))
    g = _matmul(hb, wb, F32, "l3_g", cols=(3 * BR, BR))

    log_g = jnp.log(1.0 - 2.0 ** (-5.0 - jnp.arange(heads, dtype=F32)))
    lg = jnp.broadcast_to(log_g[:, None, None], (heads, 1, V7X_LANES))
    cw = hps * dk
    spec = pl.BlockSpec((1, C, cw), lambda b, h, c: (b, c, h))
    blkbytes = (2 * _nbytes((C, cw), F32) + 3 * _nbytes((C, cw), BF16)
                + hps * (4 * _nbytes((C, C), F32) + 2 * _nbytes((dk, dk), F32)))
    return pl.pallas_call(
        functools.partial(_ret_body, hps=hps, dk=dk),
        grid=(B, heads // hps, T_pad // C),
        in_specs=[pl.BlockSpec((hps, 1, V7X_LANES), lambda b, h, c: (h, 0, 0)), spec, spec, spec, spec],
        out_specs=spec,
        out_shape=jax.ShapeDtypeStruct((B, T_pad, BR), BF16),
        scratch_shapes=[pltpu.VMEM((hps, dk, dk), F32), pltpu.VMEM((hps, C, C), F32)],
        compiler_params=_params(("parallel", "parallel", "arbitrary"), blkbytes),
        name="ret_mix",
    )(lg, q.reshape(B, T_pad, BR), k.reshape(B, T_pad, BR), v.reshape(B, T_pad, BR),
      g.reshape(B, T_pad, BR)).reshape(B * T_pad, BR)


RET_HEADS = 16


def kernel(x, meta_tokens, l0_w_in, l0_conv_w, l0_conv_b, l0_w_a, l0_b_a, l0_w_x, l0_b_x, l0_lam, l0_w_out, l0_ln_g, l0_ln_b, l1_w_in, l1_w_grp, l1_scale, l1_w_out, l1_ln_g, l1_ln_b, l2_w_in, l2_q_norm, l2_w_uq, l2_kv_norm, l2_w_ukv, l2_w_out, l2_ln_g, l2_ln_b, l3_w_in, l3_w_out, l3_ln_g, l3_ln_b):
    B, S, D = x.shape
    n_meta = meta_tokens.shape[0]
    T = n_meta + S
    T_pad = -(-T // TIME_ALIGN) * TIME_ALIGN
    M = B * T_pad

    meta = jnp.broadcast_to(meta_tokens.astype(x.dtype)[None], (B, n_meta, D))
    h = jnp.concatenate([meta, x, jnp.zeros((B, T_pad - T, D), x.dtype)], axis=1).reshape(M, D)
    hb = h.astype(BF16)

    ug = _matmul(hb, l0_w_in.astype(BF16), F32, "l0_in")
    y = _rglru_mix(ug, l0_conv_w, l0_conv_b, l0_w_a, l0_b_a, l0_w_x, l0_b_x, l0_lam, B, T_pad)
    z = _out_proj(y.reshape(M, -1), l0_w_out.astype(BF16), (h,), "l0_out_mm")
    hb, mu, rs = _layer_norm(z, l0_ln_g, l0_ln_b, "l0_out_ln")
    res = (z, mu, rs, l0_ln_g, l0_ln_b)

    ug = _matmul(hb, l1_w_in.astype(BF16), F32, "l1_in")
    y = _pool_mix(ug, l1_w_grp, l1_scale, B, T_pad)
    z = _out_proj(y.reshape(M, -1), l1_w_out.astype(BF16), res, "l1_out_mm")
    hb, mu, rs = _layer_norm(z, l1_ln_g, l1_ln_b, "l1_out_ln")
    res = (z, mu, rs, l1_ln_g, l1_ln_b)

    y = _mla_mix(hb, l2_w_in, l2_q_norm, l2_w_uq, l2_kv_norm, l2_w_ukv, B, T_pad)
    z = _out_proj(y, l2_w_out.astype(BF16), res, "l2_out_mm")
    hb, mu, rs = _layer_norm(z, l2_ln_g, l2_ln_b, "l2_out_ln")
    res = (z, mu, rs, l2_ln_g, l2_ln_b)

    y = _ret_mix(hb, l3_w_in, RET_HEADS, B, T_pad)
    z = _out_proj(y, l3_w_out.astype(BF16), res, "l3_out_mm")
    return _layer_norm_final(z, l3_ln_g, l3_ln_b, B, T_pad, n_meta, S, "l3_out_ln")
```

```python
import functools

import jax
import jax.numpy as jnp
from jax import lax
from jax.experimental import pallas as pl
from jax.experimental.pallas import tpu as pltpu

F32 = jnp.float32
BF16 = jnp.bfloat16

DEPTH = 4
ALPHA = (2.0 * DEPTH) ** 0.25
LN_EPS = 1e-5
RMS_EPS = 1e-6
ROPE_BASE = 10000.0
LRU_C = 8.0
POOL_WINDOWS = (2, 4, 8, 16)
MLA_NOPE = 128
MLA_ROPE = 64
LOG2_E = 1.4426950408889634

V7X_LANES = 128
V7X_SUBLANES = 8
V7X_BF16_ROWS = 16
V7X_SCOPED_VMEM_MAX = 60000 * 1024
V7X_VMEM_INTERNAL = 12 * 1024 * 1024

TIME_ALIGN = 128


def _pick_tile(n, target, mult):
    best = None
    for d in range(mult, min(n, target) + 1, mult):
        if n % d == 0:
            best = d
    return n if best is None else best


def _params(semantics, block_bytes):
    limit = min(2 * block_bytes + V7X_VMEM_INTERNAL, V7X_SCOPED_VMEM_MAX)
    return pltpu.CompilerParams(dimension_semantics=semantics, vmem_limit_bytes=int(limit))


def _nbytes(shape, dtype):
    n = 1
    for s in shape:
        n *= s
    return n * jnp.dtype(dtype).itemsize


def _sigmoid(x):
    return 1.0 / (1.0 + jnp.exp(-x))


def _silu(x):
    return x * _sigmoid(x)


def _mm_body(a_ref, w_ref, o_ref):
    o_ref[...] = jnp.dot(a_ref[...], w_ref[...], preferred_element_type=F32).astype(o_ref.dtype)


def _matmul(a, w, out_dtype, name, cols=None, tm_target=1056, tn_target=1024):
    M, K = a.shape
    start, N = (0, w.shape[1]) if cols is None else cols
    tm = _pick_tile(M, tm_target, V7X_BF16_ROWS)
    tn = _pick_tile(N, tn_target, V7X_LANES)
    assert start % tn == 0
    j0 = start // tn
    blk = _nbytes((tm, K), a.dtype) + _nbytes((K, tn), w.dtype) + _nbytes((tm, tn), out_dtype)
    return pl.pallas_call(
        _mm_body,
        grid=(M // tm, N // tn),
        in_specs=[pl.BlockSpec((tm, K), lambda i, j: (i, 0)),
                  pl.BlockSpec((K, tn), lambda i, j: (0, j + j0))],
        out_specs=pl.BlockSpec((tm, tn), lambda i, j: (i, j)),
        out_shape=jax.ShapeDtypeStruct((M, N), out_dtype),
        compiler_params=_params(("parallel", "arbitrary"), blk),
        name=name,
    )(a, w)


def _mm_res_body(y_ref, w_ref, h_ref, o_ref):
    o_ref[...] = ALPHA * h_ref[...] + jnp.dot(y_ref[...], w_ref[...], preferred_element_type=F32)


def _mm_lnres_body(y_ref, w_ref, z_ref, mu_ref, rs_ref, g_ref, b_ref, o_ref):
    h = (z_ref[...] - mu_ref[:, 0:1]) * rs_ref[:, 0:1] * g_ref[...] + b_ref[...]
    o_ref[...] = ALPHA * h + jnp.dot(y_ref[...], w_ref[...], preferred_element_type=F32)


def _out_proj(y, w, res, name, tm_target=1056, tn_target=512):
    M, K = y.shape
    N = w.shape[1]
    tm = _pick_tile(M, tm_target, V7X_BF16_ROWS)
    tn = _pick_tile(N, tn_target, V7X_LANES)
    blk = _nbytes((tm, K), BF16) + _nbytes((K, tn), BF16) + 3 * _nbytes((tm, tn), F32)
    tile = pl.BlockSpec((tm, tn), lambda i, j: (i, j))
    in_specs = [pl.BlockSpec((tm, K), lambda i, j: (i, 0)), pl.BlockSpec((K, tn), lambda i, j: (0, j)), tile]
    if len(res) == 1:
        body, args = _mm_res_body, (y, w, res[0])
    else:
        z, mu, rs, gamma, beta = res
        stat = pl.BlockSpec((tm, V7X_LANES), lambda i, j: (i, 0))
        vec = pl.BlockSpec((1, tn), lambda i, j: (0, j))
        in_specs += [stat, stat, vec, vec]
        body, args = _mm_lnres_body, (y, w, z, mu, rs, gamma.reshape(1, N), beta.reshape(1, N))
    return pl.pallas_call(
        body,
        grid=(M // tm, N // tn),
        in_specs=in_specs,
        out_specs=tile,
        out_shape=jax.ShapeDtypeStruct((M, N), F32),
        compiler_params=_params(("parallel", "arbitrary"), blk),
        name=name,
    )(*args)


LN_ROWS = 16
LN_UNROLL = 4


def _row_sum(x):
    w = x.shape[1]
    while w > V7X_LANES and w % (2 * V7X_LANES) == 0:
        w //= 2
        x = x[:, :w] + x[:, w:]
    return jnp.sum(x, axis=1, keepdims=True)


def _ln_rows(r):
    return pl.ds(pl.multiple_of(r * LN_ROWS, LN_ROWS), LN_ROWS)


def _ln_stats(load, n, mu_ref, rs_ref, d_model):
    inv_d = 1.0 / d_model
    stat = (LN_ROWS, V7X_LANES)

    def mean_pass(r, carry):
        mu_ref[_ln_rows(r), :] = jnp.broadcast_to(_row_sum(load(r)) * inv_d, stat)
        return carry

    def var_pass(r, carry):
        d = load(r) - mu_ref[_ln_rows(r), 0:1]
        rs_ref[_ln_rows(r), :] = jnp.broadcast_to(lax.rsqrt(_row_sum(d * d) * inv_d + LN_EPS), stat)
        return carry

    lax.fori_loop(0, n, mean_pass, 0, unroll=LN_UNROLL)
    lax.fori_loop(0, n, var_pass, 0, unroll=LN_UNROLL)


def _ln_apply(zr, r, mu_ref, rs_ref, g_ref, b_ref):
    return (zr - mu_ref[_ln_rows(r), 0:1]) * rs_ref[_ln_rows(r), 0:1] * g_ref[...] + b_ref[...]


def _ln_body(z_ref, g_ref, b_ref, ohb_ref, mu_ref, rs_ref):
    tr, D = z_ref.shape
    n = tr // LN_ROWS

    def load(r):
        return z_ref[_ln_rows(r), :]

    _ln_stats(load, n, mu_ref, rs_ref, D)

    def norm_pass(r, carry):
        ohb_ref[_ln_rows(r), :] = _ln_apply(load(r), r, mu_ref, rs_ref, g_ref, b_ref).astype(BF16)
        return carry

    lax.fori_loop(0, n, norm_pass, 0, unroll=2)


def _layer_norm(z, gamma, beta, name, tr_target=256):
    M, D = z.shape
    tr = _pick_tile(M, tr_target, LN_ROWS * LN_UNROLL)
    blk = _nbytes((tr, D), F32) + _nbytes((tr, D), BF16)
    stat = pl.BlockSpec((tr, V7X_LANES), lambda i: (i, 0))
    return pl.pallas_call(
        _ln_body,
        grid=(M // tr,),
        in_specs=[pl.BlockSpec((tr, D), lambda i: (i, 0)),
                  pl.BlockSpec((1, D), lambda i: (0, 0)),
                  pl.BlockSpec((1, D), lambda i: (0, 0))],
        out_specs=[pl.BlockSpec((tr, D), lambda i: (i, 0)), stat, stat],
        out_shape=[jax.ShapeDtypeStruct((M, D), BF16),
                   jax.ShapeDtypeStruct((M, V7X_LANES), F32),
                   jax.ShapeDtypeStruct((M, V7X_LANES), F32)],
        compiler_params=_params(("parallel",), blk),
        name=name,
    )(z, gamma.reshape(1, D), beta.reshape(1, D))


def _ln_final_body(za_ref, zb_ref, g_ref, b_ref, o_ref, mu_ref, rs_ref, *, shift):
    tr, D = o_ref.shape[1], o_ref.shape[2]
    n = tr // LN_ROWS

    def load(r):
        return za_ref[0, _ln_rows(r), :]

    _ln_stats(load, n, mu_ref, rs_ref, D)

    def norm_pass(r, carry):
        o_ref[0, _ln_rows(r - shift), :] = _ln_apply(load(r), r, mu_ref, rs_ref, g_ref, b_ref)
        return carry

    lax.fori_loop(shift, n, norm_pass, 0)
    for c in range(shift):
        zr = zb_ref[0, c * LN_ROWS:(c + 1) * LN_ROWS, :]
        mu = _row_sum(zr) * (1.0 / D)
        d = zr - mu
        rs = lax.rsqrt(_row_sum(d * d) * (1.0 / D) + LN_EPS)
        o_ref[0, (n - shift + c) * LN_ROWS:(n - shift + c + 1) * LN_ROWS, :] = d * rs * g_ref[...] + b_ref[...]


def _layer_norm_final(z, gamma, beta, B, T_pad, n_meta, S, name, tr_target=256):
    D = z.shape[1]
    assert n_meta % LN_ROWS == 0
    tr = _pick_tile(S, tr_target, LN_ROWS * LN_UNROLL)
    assert tr % n_meta == 0
    per = tr // n_meta
    blk = 2 * _nbytes((tr, D), F32) + _nbytes((n_meta, D), F32)
    z3 = z.reshape(B, T_pad, D)
    return pl.pallas_call(
        functools.partial(_ln_final_body, shift=n_meta // LN_ROWS),
        grid=(B, S // tr),
        in_specs=[pl.BlockSpec((1, tr, D), lambda b, t: (b, t, 0)),
                  pl.BlockSpec((1, n_meta, D), lambda b, t: (b, (t + 1) * per, 0)),
                  pl.BlockSpec((1, D), lambda b, t: (0, 0)),
                  pl.BlockSpec((1, D), lambda b, t: (0, 0))],
        out_specs=pl.BlockSpec((1, tr, D), lambda b, t: (b, t, 0)),
        out_shape=jax.ShapeDtypeStruct((B, S, D), F32),
        scratch_shapes=[pltpu.VMEM((tr, V7X_LANES), F32), pltpu.VMEM((tr, V7X_LANES), F32)],
        compiler_params=_params(("parallel", "parallel"), blk),
        name=name,
    )(z3, z3, gamma.reshape(1, D), beta.reshape(1, D))


SEG = V7X_SUBLANES
GROUP = SEG * SEG


def _sub_shift(v, d, fill, sub):
    return jnp.where(sub >= d, pltpu.roll(v, d, 0), fill)


def _rglru_body(*refs, conv_w, hpb, blk, nl):
    u_refs = refs[:nl]
    g_ref, cw_ref, cb_ref, wg_ref, bg_ref, lam_ref, y_ref = refs[nl:nl + 7]
    ucb, sa, sx, hs, tail, hcar = refs[nl + 7:]
    t = pl.program_id(2)
    tt = g_ref.shape[0]
    ngroup = tt // GROUP
    sub = lax.broadcasted_iota(jnp.int32, (SEG, V7X_LANES), 0)
    tile = (SEG, V7X_LANES)

    @pl.when(t == 0)
    def _():
        tail[...] = jnp.zeros_like(tail)
        hcar[...] = jnp.zeros_like(hcar)

    for c in range(nl):
        lanes = slice(c * V7X_LANES, (c + 1) * V7X_LANES)
        wk = [jnp.broadcast_to(cw_ref[k:k + 1, lanes], tile) for k in range(conv_w)]
        cb = jnp.broadcast_to(cb_ref[:, lanes], tile)
        prev = [tail[c, m] for m in range(conv_w - 1)]
        for gi in range(ngroup):
            cur = [u_refs[c][pl.ds(gi * GROUP + j, SEG, stride=SEG), :] for j in range(SEG)]
            ext = [jnp.where(sub == 0, pltpu.roll(prev[m], 1, 0), pltpu.roll(cur[SEG - (conv_w - 1) + m], 1, 0))
                   for m in range(conv_w - 1)]
            seq = ext + cur
            for j in range(SEG):
                acc = cb
                for k in range(conv_w):
                    acc = acc + wk[k] * seq[j + k]
                ucb[gi * GROUP + j * SEG:gi * GROUP + (j + 1) * SEG, lanes] = acc
            prev = cur[SEG - (conv_w - 1):]
        for m in range(conv_w - 1):
            tail[c, m] = prev[m]

    lam = lam_ref[...]
    logsig = -(jnp.maximum(-lam, 0.0) + jnp.log1p(jnp.exp(-jnp.abs(lam))))
    for hh in range(hpb):
        cols = slice(hh * blk, (hh + 1) * blk)
        ucs = ucb[:, cols]
        gate = jnp.dot(ucs.astype(BF16), wg_ref[hh], preferred_element_type=F32) + bg_ref[hh]
        r = _sigmoid(gate[:, :blk])
        i = _sigmoid(gate[:, blk:])
        a = jnp.exp(LRU_C * r * logsig[:, cols])
        sa[:, cols] = a
        sx[:, cols] = (ucs * i) * jnp.sqrt(1.0 - a * a)

    for c in range(nl):
        lanes = slice(c * V7X_LANES, (c + 1) * V7X_LANES)
        h_in = hcar[c]
        for gi in range(ngroup):
            hloc, ploc = [], []
            for j in range(SEG):
                rows = slice(gi * GROUP + j * SEG, gi * GROUP + (j + 1) * SEG)
                a = sa[rows, lanes]
                x = sx[rows, lanes]
                hloc.append(x if j == 0 else a * hloc[-1] + x)
                ploc.append(a if j == 0 else a * ploc[-1])
            at, xt = ploc[-1], hloc[-1]
            d = 1
            while d < SEG:
                xt = at * _sub_shift(xt, d, 0.0, sub) + xt
                at = at * _sub_shift(at, d, 1.0, sub)
                d *= 2
            carry = jnp.where(sub == 0, h_in, pltpu.roll(xt, 1, 0) + pltpu.roll(at, 1, 0) * h_in)
            for j in range(SEG):
                h = hloc[j] + ploc[j] * carry
                hs[c, pl.ds(gi * GROUP + j, SEG, stride=SEG), :] = h
            h_in = jnp.broadcast_to(h[SEG - 1:SEG, :], tile)
        hcar[c] = h_in

    for c in range(nl):
        lanes = slice(c * V7X_LANES, (c + 1) * V7X_LANES)
        y_ref[:, lanes] = (hs[c] * _silu(g_ref[:, lanes])).astype(BF16)


def _rglru_mix(ug, conv_w, conv_b, w_a, b_a, w_x, b_x, lam, B, T_pad):
    BR = conv_b.shape[0]
    heads, blk, _ = w_a.shape
    K = conv_w.shape[0]
    hpb = 2 if heads % 2 == 0 else 1
    cw = hpb * blk
    nl = cw // V7X_LANES
    ncb = BR // cw
    tt = _pick_tile(T_pad, 384, GROUP)
    nT = T_pad // tt
    wg = jnp.concatenate([w_a, w_x], axis=-1).astype(BF16)
    bg = jnp.concatenate([b_a.reshape(heads, 1, blk), b_x.reshape(heads, 1, blk)], axis=-1)
    body = functools.partial(_rglru_body, conv_w=K, hpb=hpb, blk=blk, nl=nl)
    blkbytes = 7 * _nbytes((tt, cw), F32) + _nbytes((hpb, blk, 2 * blk), BF16)

    def u_spec(l):
        return pl.BlockSpec((tt, V7X_LANES), lambda b, c, t: (b * nT + t, c * nl + l))

    return pl.pallas_call(
        body,
        grid=(B, ncb, nT),
        in_specs=[u_spec(l) for l in range(nl)] + [
            pl.BlockSpec((tt, cw), lambda b, c, t: (b * nT + t, c + ncb)),
            pl.BlockSpec((K, cw), lambda b, c, t: (0, c)),
            pl.BlockSpec((1, cw), lambda b, c, t: (0, c)),
            pl.BlockSpec((hpb, blk, 2 * blk), lambda b, c, t: (c, 0, 0)),
            pl.BlockSpec((hpb, 1, 2 * blk), lambda b, c, t: (c, 0, 0)),
            pl.BlockSpec((1, cw), lambda b, c, t: (0, c))],
        out_specs=pl.BlockSpec((tt, cw), lambda b, c, t: (b * nT + t, c)),
        out_shape=jax.ShapeDtypeStruct((B * T_pad, BR), BF16),
        scratch_shapes=[pltpu.VMEM((tt, cw), F32),
                        pltpu.VMEM((tt, cw), F32),
                        pltpu.VMEM((tt, cw), F32),
                        pltpu.VMEM((nl, tt, V7X_LANES), F32),
                        pltpu.VMEM((nl, K - 1, SEG, V7X_LANES), F32),
                        pltpu.VMEM((nl, SEG, V7X_LANES), F32)],
        compiler_params=_params(("parallel", "parallel", "arbitrary"), blkbytes),
        name="rglru_mix",
    )(*([ug] * nl), ug, conv_w.reshape(K, BR), conv_b.reshape(1, BR), wg, bg, lam.reshape(1, BR))


POOL_HIST = 24


def _pool_body(u_ref, g_ref, w_ref, sc_ref, y_ref, ubuf, sbuf, pbuf, *, windows):
    grp = pl.program_id(0)
    t = pl.program_id(2)
    tt = u_ref.shape[1]
    gw = u_ref.shape[2]
    n = POOL_HIST + tt

    @pl.when(t == 0)
    def _():
        ubuf[0:POOL_HIST, :] = jnp.zeros((POOL_HIST, gw), F32)

    sbuf[0:V7X_SUBLANES, :] = jnp.zeros((V7X_SUBLANES, gw), F32)
    ubuf[POOL_HIST:n, :] = u_ref[0]
    pos = lax.broadcasted_iota(jnp.int32, (tt, 1), 0) + t * tt + 1

    for gi, w in enumerate(windows):
        @pl.when(grp == gi)
        def _(w=w):
            lo = V7X_SUBLANES
            s = ubuf[lo:n, :] + ubuf[lo - 1:n - 1, :]
            d = 2
            while d < w:
                sbuf[lo:n, :] = s
                s = sbuf[lo:n, :] + sbuf[lo - d:n - d, :]
                d *= 2
            wsum = s[POOL_HIST - lo:, :]
            cnt = jnp.minimum(pos, w).astype(F32)
            pbuf[...] = (wsum / cnt - u_ref[0]).astype(BF16)

    ubuf[0:POOL_HIST, :] = ubuf[tt:n, :]
    mixed = jnp.dot(pbuf[...], w_ref[0], preferred_element_type=F32) * sc_ref[...]
    y_ref[0] = (mixed * _silu(g_ref[0])).astype(BF16)


def _pool_mix(ug, w_grp, scale, B, T_pad):
    ngrp, gw, _ = w_grp.shape
    BR = ngrp * gw
    tt = _pick_tile(T_pad, 384, V7X_BF16_ROWS)
    ug3 = ug.reshape(B, T_pad, 2 * BR)
    body = functools.partial(_pool_body, windows=POOL_WINDOWS[:ngrp])
    blkbytes = 5 * _nbytes((tt, gw), F32) + _nbytes((gw, gw), BF16)
    return pl.pallas_call(
        body,
        grid=(ngrp, B, T_pad // tt),
        in_specs=[pl.BlockSpec((1, tt, gw), lambda g, b, t: (b, t, g)),
                  pl.BlockSpec((1, tt, gw), lambda g, b, t: (b, t, g + ngrp)),
                  pl.BlockSpec((1, gw, gw), lambda g, b, t: (g, 0, 0)),
                  pl.BlockSpec((1, gw), lambda g, b, t: (0, g))],
        out_specs=pl.BlockSpec((1, tt, gw), lambda g, b, t: (b, t, g)),
        out_shape=jax.ShapeDtypeStruct((B, T_pad, BR), BF16),
        scratch_shapes=[pltpu.VMEM((POOL_HIST + tt, gw), F32),
                        pltpu.VMEM((POOL_HIST + tt, gw), F32),
                        pltpu.VMEM((tt, gw), BF16)],
        compiler_params=_params(("parallel", "parallel", "arbitrary"), blkbytes),
        name="pool_mix",
    )(ug3, ug3, w_grp.astype(BF16), scale.reshape(1, BR))


def _rope_pairs(x, c_ref, s_ref):
    half = MLA_ROPE // 2
    lane = lax.broadcasted_iota(jnp.int32, x.shape, 1)
    partner = jnp.where(lane % MLA_ROPE < half,
                        pltpu.roll(x, V7X_LANES - half, 1), pltpu.roll(x, half, 1))
    return x * c_ref[...] + partner * s_ref[...]


def _rms_scale(x, g):
    return x * lax.rsqrt(jnp.mean(jnp.square(x), axis=-1, keepdims=True) + RMS_EPS) * g


def _qproj_body(c_ref, n_ref, w_ref, cos_ref, sin_ref, q_ref, xn_ref, *, ppb):
    @pl.when(pl.program_id(1) == 0)
    def _():
        xn_ref[...] = _rms_scale(c_ref[...], n_ref[...]).astype(BF16)

    res = jnp.dot(xn_ref[...], w_ref[...], preferred_element_type=F32)
    nn = 2 * MLA_NOPE
    pw = nn + 2 * MLA_ROPE
    for p in range(ppb):
        q_ref[:, p * pw:p * pw + nn] = res[:, p * pw:p * pw + nn].astype(BF16)
        q_ref[:, p * pw + nn:(p + 1) * pw] = _rope_pairs(
            res[:, p * pw + nn:(p + 1) * pw], cos_ref, sin_ref).astype(BF16)


def _kvproj_body(c_ref, kr_ref, n_ref, w_ref, cos_ref, sin_ref, kv_ref, krd_ref, xn_ref):
    @pl.when(pl.program_id(1) == 0)
    def _():
        xn_ref[...] = _rms_scale(c_ref[...], n_ref[...]).astype(BF16)
        rot = _rope_pairs(kr_ref[...], cos_ref, sin_ref)
        lane = lax.broadcasted_iota(jnp.int32, rot.shape, 1)
        krd_ref[...] = jnp.where(lane < MLA_ROPE, rot, pltpu.roll(rot, MLA_ROPE, 1)).astype(BF16)

    kv_ref[...] = jnp.dot(xn_ref[...], w_ref[...], preferred_element_type=F32).astype(BF16)


def _attn_body(q_ref, kn_ref, v_ref, kr_ref, g_ref, y_ref, m_ref, acc_ref, *, scale2, pps):
    qi = pl.program_id(2)
    tq = q_ref.shape[1]
    wide = 2 * tq
    nn = 2 * MLA_NOPE
    pw = nn + 2 * MLA_ROPE
    heads = [(pp, hh) for pp in range(pps) for hh in range(2)]
    lane = lax.broadcasted_iota(jnp.int32, (tq, V7X_LANES), 1)
    qs = []
    for pp, hh in heads:
        qr = q_ref[0, :, pp * pw + nn:(pp + 1) * pw]
        mine = (lane < MLA_ROPE) if hh == 0 else (lane >= MLA_ROPE)
        nope = q_ref[0, :, pp * pw + hh * MLA_NOPE:pp * pw + (hh + 1) * MLA_NOPE]
        qs.append(jnp.concatenate([nope, jnp.where(mine, qr, jnp.zeros_like(qr))], axis=1))

    m_ref[...] = jnp.full_like(m_ref, -jnp.inf)
    acc_ref[...] = jnp.zeros_like(acc_ref)

    def block(k0, tk, mask_shift):
        krope = kr_ref[0, pl.ds(k0, tk), :]
        ones = jnp.ones((tk, V7X_LANES), BF16)
        nc = tk // V7X_LANES
        for h, (pp, hh) in enumerate(heads):
            hc = slice((2 * pp + hh) * MLA_NOPE, (2 * pp + hh + 1) * MLA_NOPE)
            kf = jnp.concatenate([kn_ref[0, pl.ds(k0, tk), hc], krope], axis=1)
            s = lax.dot_general(qs[h], kf, (((1,), (1,)), ((), ())),
                                preferred_element_type=F32) * scale2
            if mask_shift is not None:
                rq = lax.broadcasted_iota(jnp.int32, (tq, tk), 0)
                ck = lax.broadcasted_iota(jnp.int32, (tq, tk), 1)
                s = jnp.where(ck <= rq + mask_shift, s, -1e30)
            cols = [s[:, c * V7X_LANES:(c + 1) * V7X_LANES] for c in range(nc)]
            mloc = cols[0]
            for c in range(1, nc):
                mloc = jnp.maximum(mloc, cols[c])
            m_old = m_ref[h]
            m_new = jnp.maximum(m_old, jnp.max(mloc, axis=1, keepdims=True))
            p = jnp.concatenate([jnp.exp2(cols[c] - m_new) for c in range(nc)], axis=1).astype(BF16)
            corr = jnp.exp2(m_old - m_new)
            vf = jnp.concatenate([v_ref[0, pl.ds(k0, tk), hc], ones], axis=1)
            pv = jnp.dot(p, vf, preferred_element_type=F32)
            acc_ref[h] = jnp.concatenate([corr, corr], axis=1) * acc_ref[h] + pv
            m_ref[h] = m_new

    nblk = qi // 2

    def loop_body(j, carry):
        block(pl.multiple_of(2 * j * wide, wide), wide, None)
        block(pl.multiple_of((2 * j + 1) * wide, wide), wide, None)
        return carry

    lax.fori_loop(0, nblk // 2, loop_body, 0)

    @pl.when(nblk % 2 == 1)
    def _():
        block(pl.multiple_of((nblk - 1) * wide, wide), wide, None)

    @pl.when(qi % 2 == 1)
    def _():
        block(pl.multiple_of((qi - 1) * tq, tq), wide, tq)

    @pl.when(qi % 2 == 0)
    def _():
        block(pl.multiple_of(qi * tq, tq), tq, 0)

    gate = _silu(g_ref[0])
    for h, (pp, hh) in enumerate(heads):
        hc = slice((2 * pp + hh) * MLA_NOPE, (2 * pp + hh + 1) * MLA_NOPE)
        acc = acc_ref[h]
        o = acc[:, :MLA_NOPE] / acc[:, MLA_NOPE:]
        y_ref[0, :, hc] = (o * gate[:, hc]).astype(BF16)


def _mla_mix(hb, w_in, q_norm, w_uq, kv_norm, w_ukv, B, T_pad):
    q_lora = q_norm.shape[0]
    kv_lora = kv_norm.shape[0]
    heads = w_uq.shape[1] // (MLA_NOPE + MLA_ROPE)
    vdim = w_ukv.shape[1] // heads - MLA_NOPE
    assert vdim == MLA_NOPE and heads % 2 == 0
    BR = heads * vdim
    npair = heads // 2
    M = hb.shape[0]
    D = hb.shape[1]

    n_in = w_in.shape[1]
    n_pad = -n_in % V7X_LANES
    w_in_p = jnp.pad(w_in, ((0, 0), (0, n_pad))).astype(BF16)
    c_all = _matmul(hb, w_in_p, F32, "mla_in", tn_target=1152)
    n_all = n_in + n_pad

    tm = _pick_tile(T_pad, 1056, V7X_BF16_ROWS)
    nt = T_pad // tm

    half = MLA_ROPE // 2
    inv = ROPE_BASE ** (-jnp.arange(0, MLA_ROPE, 2, dtype=F32) / MLA_ROPE)
    ang = jnp.arange(T_pad, dtype=F32)[:, None] * inv[None, :]
    cos_t = jnp.tile(jnp.cos(ang), (1, 4))
    sin_t = jnp.tile(jnp.concatenate([-jnp.sin(ang), jnp.sin(ang)], axis=1), (1, 2))
    del half

    hd = MLA_NOPE + MLA_ROPE
    wq = w_uq.reshape(q_lora, npair, 2, hd)
    wq = jnp.concatenate([wq[..., 0, :MLA_NOPE], wq[..., 1, :MLA_NOPE],
                          wq[..., 0, MLA_NOPE:], wq[..., 1, MLA_NOPE:]], axis=-1)
    pw = 2 * hd
    wq = wq.reshape(q_lora, npair * pw).astype(BF16)
    cq_blk = BR // q_lora
    ppb = _pick_tile(npair, 4, 1)
    qw = ppb * pw
    q = pl.pallas_call(
        functools.partial(_qproj_body, ppb=ppb),
        grid=(M // tm, npair // ppb),
        in_specs=[pl.BlockSpec((tm, q_lora), lambda i, j: (i, cq_blk)),
                  pl.BlockSpec((1, q_lora), lambda i, j: (0, 0)),
                  pl.BlockSpec((q_lora, qw), lambda i, j: (0, j)),
                  pl.BlockSpec((tm, V7X_LANES), lambda i, j: (i % nt, 0)),
                  pl.BlockSpec((tm, V7X_LANES), lambda i, j: (i % nt, 0))],
        out_specs=pl.BlockSpec((tm, qw), lambda i, j: (i, j)),
        out_shape=jax.ShapeDtypeStruct((M, npair * pw), BF16),
        scratch_shapes=[pltpu.VMEM((tm, q_lora), BF16)],
        compiler_params=_params(("parallel", "arbitrary"),
                                _nbytes((tm, q_lora), F32) + _nbytes((q_lora, qw), BF16)
                                + 2 * _nbytes((tm, qw), F32)),
        name="mla_qproj",
    )(c_all, q_norm.reshape(1, q_lora), wq, cos_t, sin_t)

    wkv = w_ukv.reshape(kv_lora, heads, MLA_NOPE + vdim)
    wkv = jnp.concatenate([wkv[..., :MLA_NOPE].reshape(kv_lora, BR),
                           wkv[..., MLA_NOPE:].reshape(kv_lora, BR)], axis=1).astype(BF16)
    tn = _pick_tile(2 * BR, 1024, V7X_LANES)
    ckv_blk = (BR + q_lora) // kv_lora
    kr_blk = (BR + q_lora + kv_lora) // V7X_LANES
    kv, krd = pl.pallas_call(
        _kvproj_body,
        grid=(M // tm, 2 * BR // tn),
        in_specs=[pl.BlockSpec((tm, kv_lora), lambda i, j: (i, ckv_blk)),
                  pl.BlockSpec((tm, V7X_LANES), lambda i, j: (i, kr_blk)),
                  pl.BlockSpec((1, kv_lora), lambda i, j: (0, 0)),
                  pl.BlockSpec((kv_lora, tn), lambda i, j: (0, j)),
                  pl.BlockSpec((tm, V7X_LANES), lambda i, j: (i % nt, 0)),
                  pl.BlockSpec((tm, V7X_LANES), lambda i, j: (i % nt, 0))],
        out_specs=[pl.BlockSpec((tm, tn), lambda i, j: (i, j)),
                   pl.BlockSpec((tm, V7X_LANES), lambda i, j: (i, 0))],
        out_shape=[jax.ShapeDtypeStruct((M, 2 * BR), BF16),
                   jax.ShapeDtypeStruct((M, V7X_LANES), BF16)],
        scratch_shapes=[pltpu.VMEM((tm, kv_lora), BF16)],
        compiler_params=_params(("parallel", "arbitrary"),
                                _nbytes((tm, kv_lora), F32) + _nbytes((kv_lora, tn), BF16)
                                + 2 * _nbytes((tm, tn), F32)),
        name="mla_kvproj",
    )(c_all, c_all, kv_norm.reshape(1, kv_lora), wkv, cos_t, sin_t)

    tq = _pick_tile(T_pad, 384, V7X_BF16_ROWS)
    pps = _pick_tile(npair, 4, 1)
    ngrp = npair // pps
    nh = 2 * pps
    pv = nh * vdim
    qw = pps * pw
    q3 = q.reshape(B, T_pad, npair * pw)
    kv3 = kv.reshape(B, T_pad, 2 * BR)
    krd3 = krd.reshape(B, T_pad, V7X_LANES)
    c3 = c_all.reshape(B, T_pad, n_all)
    body = functools.partial(_attn_body, scale2=float(hd) ** -0.5 * LOG2_E, pps=pps)
    blkbytes = (_nbytes((tq, qw), BF16) + 2 * _nbytes((T_pad, pv), BF16) + _nbytes((T_pad, V7X_LANES), BF16)
                + _nbytes((tq, pv), F32) + _nbytes((tq, pv), BF16) + 2 * nh * _nbytes((tq, 2 * tq), F32))
    return pl.pallas_call(
        body,
        grid=(B, ngrp, T_pad // tq),
        in_specs=[pl.BlockSpec((1, tq, qw), lambda b, p, i: (b, i, p)),
                  pl.BlockSpec((1, T_pad, pv), lambda b, p, i: (b, 0, p)),
                  pl.BlockSpec((1, T_pad, pv), lambda b, p, i: (b, 0, p + ngrp)),
                  pl.BlockSpec((1, T_pad, V7X_LANES), lambda b, p, i: (b, 0, 0)),
                  pl.BlockSpec((1, tq, pv), lambda b, p, i: (b, i, p))],
        out_specs=pl.BlockSpec((1, tq, pv), lambda b, p, i: (b, i, p)),
        out_shape=jax.ShapeDtypeStruct((B, T_pad, BR), BF16),
        scratch_shapes=[pltpu.VMEM((nh, tq, V7X_LANES), F32),
                        pltpu.VMEM((nh, tq, 2 * vdim), F32)],
        compiler_params=_params(("parallel", "parallel", "arbitrary"), blkbytes),
        name="mla_attn",
    )(q3, kv3, kv3, krd3, c3).reshape(M, BR)


def _mm_rope_body(a_ref, w_ref, cos_ref, sin_ref, o_ref, *, dk, scale):
    res = jnp.dot(a_ref[...], w_ref[...], preferred_element_type=F32)
    cos = cos_ref[...]
    sin = sin_ref[...]
    hd = dk // 2
    for h in range(o_ref.shape[1] // dk):
        x1 = res[:, h * dk:h * dk + hd]
        x2 = res[:, h * dk + hd:(h + 1) * dk]
        o_ref[:, h * dk:h * dk + hd] = ((x1 * cos - x2 * sin) * scale).astype(o_ref.dtype)
        o_ref[:, h * dk + hd:(h + 1) * dk] = ((x2 * cos + x1 * sin) * scale).astype(o_ref.dtype)


def _matmul_rope(a, w, cols, cos_t, sin_t, dk, scale, out_dtype, name, tm_target=1056, tn_target=1024):
    M, K = a.shape
    start, N = cols
    T_pad = cos_t.shape[0]
    tm = _pick_tile(T_pad, tm_target, V7X_BF16_ROWS)
    tn = _pick_tile(N, tn_target, dk)
    assert start % tn == 0
    j0 = start // tn
    nt = T_pad // tm
    blk = (_nbytes((tm, K), a.dtype) + _nbytes((K, tn), w.dtype) + 2 * _nbytes((tm, tn), F32)
           + 2 * _nbytes((tm, dk // 2), F32))
    return pl.pallas_call(
        functools.partial(_mm_rope_body, dk=dk, scale=scale),
        grid=(M // tm, N // tn),
        in_specs=[pl.BlockSpec((tm, K), lambda i, j: (i, 0)),
                  pl.BlockSpec((K, tn), lambda i, j: (0, j + j0)),
                  pl.BlockSpec((tm, dk // 2), lambda i, j: (i % nt, 0)),
                  pl.BlockSpec((tm, dk // 2), lambda i, j: (i % nt, 0))],
        out_specs=pl.BlockSpec((tm, tn), lambda i, j: (i, j)),
        out_shape=jax.ShapeDtypeStruct((M, N), out_dtype),
        compiler_params=_params(("parallel", "arbitrary"), blk),
        name=name,
    )(a, w, cos_t, sin_t)


def _ret_body(lg_ref, q_ref, k_ref, v_ref, g_ref, y_ref, st_ref, dec_ref, *, hps, dk):
    c = pl.program_id(2)
    C = q_ref.shape[1]

    @pl.when(c == 0)
    def _():
        st_ref[...] = jnp.zeros_like(st_ref)
        ri = lax.broadcasted_iota(jnp.int32, (C, C), 0)
        ci = lax.broadcasted_iota(jnp.int32, (C, C), 1)
        diff = (ri - ci).astype(F32)
        for h in range(hps):
            dec_ref[h] = jnp.where(diff >= 0, jnp.exp(jnp.maximum(diff, 0.0) * lg_ref[h, :, 0:1]), 0.0)

    jcol = lax.broadcasted_iota(jnp.int32, (C, 1), 0).astype(F32)
    for h in range(hps):
        cols = slice(h * dk, (h + 1) * dk)
        lg = lg_ref[h, :, 0:1]
        qb = q_ref[0, :, cols]
        kf = k_ref[0, :, cols]
        vb = v_ref[0, :, cols]
        s = lax.dot_general(qb, kf.astype(BF16), (((1,), (1,)), ((), ())), preferred_element_type=F32)
        intra = jnp.dot((s * dec_ref[h]).astype(BF16), vb, preferred_element_type=F32)
        state = st_ref[h]
        inter = jnp.dot(qb, state.astype(BF16), preferred_element_type=F32) * jnp.exp((jcol + 1.0) * lg)
        kw = (kf * jnp.exp((C - 1.0 - jcol) * lg)).astype(BF16)
        st_ref[h] = jnp.exp(C * lg) * state + lax.dot_general(
            kw, vb, (((0,), (0,)), ((), ())), preferred_element_type=F32)
        o = intra + inter
        mu = jnp.mean(o, axis=-1, keepdims=True)
        var = jnp.mean(jnp.square(o - mu), axis=-1, keepdims=True)
        o = (o - mu) * lax.rsqrt(var + LN_EPS)
        y_ref[0, :, cols] = (o * _silu(g_ref[0, :, cols])).astype(BF16)


def _ret_mix(hb, w_in, heads, B, T_pad):
    BR = w_in.shape[1] // 4
    dk = BR // heads
    C = _pick_tile(T_pad, 384, V7X_BF16_ROWS)
    hps = 2 if heads % 2 == 0 else 1
    inv = ROPE_BASE ** (-jnp.arange(0, dk, 2, dtype=F32) / dk)
    ang = jnp.arange(T_pad, dtype=F32)[:, None] * inv[None, :]
    cos_t = jnp.cos(ang)
    sin_t = jnp.sin(ang)
    wb = w_in.astype(BF16)
    q = _matmul_rope(hb, wb, (0, BR), cos_t, sin_t, dk, 1.0, BF16, "l3_q")
    k = _matmul_rope(hb, wb, (BR, BR), cos_t, sin_t, dk, float(dk) ** -0.5, F32, "l3_k")
    v = _matmul(hb, wb, BF16, "l3_v", cols=(2 * BR, BR))
    g = _matmul(hb, wb, F32, "l3_g", cols=(3 * BR, BR))

    log_g = jnp.log(1.0 - 2.0 ** (-5.0 - jnp.arange(heads, dtype=F32)))
    lg = jnp.broadcast_to(log_g[:, None, None], (heads, 1, V7X_LANES))
    cw = hps * dk
    spec = pl.BlockSpec((1, C, cw), lambda b, h, c: (b, c, h))
    blkbytes = (2 * _nbytes((C, cw), F32) + 3 * _nbytes((C, cw), BF16)
                + hps * (4 * _nbytes((C, C), F32) + 2 * _nbytes((dk, dk), F32)))
    return pl.pallas_call(
        functools.partial(_ret_body, hps=hps, dk=dk),
        grid=(B, heads // hps, T_pad // C),
        in_specs=[pl.BlockSpec((hps, 1, V7X_LANES), lambda b, h, c: (h, 0, 0)), spec, spec, spec, spec],
        out_specs=spec,
        out_shape=jax.ShapeDtypeStruct((B, T_pad, BR), BF16),
        scratch_shapes=[pltpu.VMEM((hps, dk, dk), F32), pltpu.VMEM((hps, C, C), F32)],
        compiler_params=_params(("parallel", "parallel", "arbitrary"), blkbytes),
        name="ret_mix",
    )(lg, q.reshape(B, T_pad, BR), k.reshape(B, T_pad, BR), v.reshape(B, T_pad, BR),
      g.reshape(B, T_pad, BR)).reshape(B * T_pad, BR)


RET_HEADS = 16


def kernel(x, meta_tokens, l0_w_in, l0_conv_w, l0_conv_b, l0_w_a, l0_b_a, l0_w_x, l0_b_x, l0_lam, l0_w_out, l0_ln_g, l0_ln_b, l1_w_in, l1_w_grp, l1_scale, l1_w_out, l1_ln_g, l1_ln_b, l2_w_in, l2_q_norm, l2_w_uq, l2_kv_norm, l2_w_ukv, l2_w_out, l2_ln_g, l2_ln_b, l3_w_in, l3_w_out, l3_ln_g, l3_ln_b):
    B, S, D = x.shape
    n_meta = meta_tokens.shape[0]
    T = n_meta + S
    T_pad = -(-T // TIME_ALIGN) * TIME_ALIGN
    M = B * T_pad

    meta = jnp.broadcast_to(meta_tokens.astype(x.dtype)[None], (B, n_meta, D))
    h = jnp.concatenate([meta, x, jnp.zeros((B, T_pad - T, D), x.dtype)], axis=1).reshape(M, D)
    hb = h.astype(BF16)

    ug = _matmul(hb, l0_w_in.astype(BF16), F32, "l0_in")
    y = _rglru_mix(ug, l0_conv_w, l0_conv_b, l0_w_a, l0_b_a, l0_w_x, l0_b_x, l0_lam, B, T_pad)
    z = _out_proj(y.reshape(M, -1), l0_w_out.astype(BF16), (h,), "l0_out_mm")
    hb, mu, rs = _layer_norm(z, l0_ln_g, l0_ln_b, "l0_out_ln")
    res = (z, mu, rs, l0_ln_g, l0_ln_b)

    ug = _matmul(hb, l1_w_in.astype(BF16), F32, "l1_in")
    y = _pool_mix(ug, l1_w_grp, l1_scale, B, T_pad)
    z = _out_proj(y.reshape(M, -1), l1_w_out.astype(BF16), res, "l1_out_mm")
    hb, mu, rs = _layer_norm(z, l1_ln_g, l1_ln_b, "l1_out_ln")
    res = (z, mu, rs, l1_ln_g, l1_ln_b)

    y = _mla_mix(hb, l2_w_in, l2_q_norm, l2_w_uq, l2_kv_norm, l2_w_ukv, B, T_pad)
    z = _out_proj(y, l2_w_out.astype(BF16), res, "l2_out_mm")
    hb, mu, rs = _layer_norm(z, l2_ln_g, l2_ln_b, "l2_out_ln")
    res = (z, mu, rs, l2_ln_g, l2_ln_b)

    y = _ret_mix(hb, l3_w_in, RET_HEADS, B, T_pad)
    z = _out_proj(y, l3_w_out.astype(BF16), res, "l3_out_mm")
    return _layer_norm_final(z, l3_ln_g, l3_ln_b, B, T_pad, n_meta, S, "l3_out_ln")
```

```python
import functools

import jax
import jax.numpy as jnp
from jax import lax
from jax.experimental import pallas as pl
from jax.experimental.pallas import tpu as pltpu

F32 = jnp.float32
BF16 = jnp.bfloat16

DEPTH = 4
ALPHA = (2.0 * DEPTH) ** 0.25
LN_EPS = 1e-5
RMS_EPS = 1e-6
ROPE_BASE = 10000.0
LRU_C = 8.0
POOL_WINDOWS = (2, 4, 8, 16)
MLA_NOPE = 128
MLA_ROPE = 64
LOG2_E = 1.4426950408889634

V7X_LANES = 128
V7X_SUBLANES = 8
V7X_BF16_ROWS = 16
V7X_SCOPED_VMEM_MAX = 60000 * 1024
V7X_VMEM_INTERNAL = 12 * 1024 * 1024

TIME_ALIGN = 128


def _pick_tile(n, target, mult):
    best = None
    for d in range(mult, min(n, target) + 1, mult):
        if n % d == 0:
            best = d
    return n if best is None else best


def _params(semantics, block_bytes):
    limit = min(2 * block_bytes + V7X_VMEM_INTERNAL, V7X_SCOPED_VMEM_MAX)
    return pltpu.CompilerParams(dimension_semantics=semantics, vmem_limit_bytes=int(limit))


def _nbytes(shape, dtype):
    n = 1
    for s in shape:
        n *= s
    return n * jnp.dtype(dtype).itemsize


def _sigmoid(x):
    return 1.0 / (1.0 + jnp.exp(-x))


def _silu(x):
    return x * _sigmoid(x)


def _cast_specs(casts, grid):
    ni, nj = grid
    nsteps = ni * nj
    in_specs, out_specs, shapes, nbytes = [], [], [], 0
    for wt in casts:
        R, C = wt.shape
        rows = R // nsteps
        assert R % nsteps == 0 and rows % V7X_BF16_ROWS == 0
        spec = pl.BlockSpec((rows, C), lambda i, j: (i * nj + j, 0))
        in_specs.append(spec)
        out_specs.append(spec)
        shapes.append(jax.ShapeDtypeStruct((R, C), BF16))
        nbytes += _nbytes((rows, C), F32) + _nbytes((rows, C), BF16)
    return in_specs, out_specs, shapes, nbytes


def _cast_blocks(src_refs, dst_refs):
    for src, dst in zip(src_refs, dst_refs):
        dst[...] = src[...].astype(BF16)


def _mm_body(a_ref, w_ref, *rest):
    o_ref = rest[len(rest) // 2]
    o_ref[...] = jnp.dot(a_ref[...], w_ref[...], preferred_element_type=F32).astype(o_ref.dtype)
    _cast_blocks(rest[:len(rest) // 2], rest[len(rest) // 2 + 1:])


def _matmul(a, w, out_dtype, name, cols=None, casts=(), tm_target=1056, tn_target=1024):
    M, K = a.shape
    start, N = (0, w.shape[1]) if cols is None else cols
    tm = _pick_tile(M, tm_target, V7X_BF16_ROWS)
    tn = _pick_tile(N, tn_target, V7X_LANES)
    assert start % tn == 0
    j0 = start // tn
    grid = (M // tm, N // tn)
    c_in, c_out, c_shape, c_bytes = _cast_specs(casts, grid)
    blk = _nbytes((tm, K), a.dtype) + _nbytes((K, tn), w.dtype) + _nbytes((tm, tn), out_dtype) + c_bytes
    outs = pl.pallas_call(
        _mm_body,
        grid=grid,
        in_specs=[pl.BlockSpec((tm, K), lambda i, j: (i, 0)),
                  pl.BlockSpec((K, tn), lambda i, j: (0, j + j0))] + c_in,
        out_specs=[pl.BlockSpec((tm, tn), lambda i, j: (i, j))] + c_out,
        out_shape=[jax.ShapeDtypeStruct((M, N), out_dtype)] + c_shape,
        compiler_params=_params(("parallel", "arbitrary"), blk),
        name=name,
    )(a, w, *casts)
    return (outs[0], tuple(outs[1:])) if casts else outs[0]


def _mm_res_body(y_ref, w_ref, h_ref, *rest):
    o_ref = rest[len(rest) // 2]
    o_ref[...] = ALPHA * h_ref[...] + jnp.dot(y_ref[...], w_ref[...], preferred_element_type=F32)
    _cast_blocks(rest[:len(rest) // 2], rest[len(rest) // 2 + 1:])


def _mm_lnres_body(y_ref, w_ref, z_ref, mu_ref, rs_ref, g_ref, b_ref, *rest):
    o_ref = rest[len(rest) // 2]
    h = (z_ref[...] - mu_ref[:, 0:1]) * rs_ref[:, 0:1] * g_ref[...] + b_ref[...]
    o_ref[...] = ALPHA * h + jnp.dot(y_ref[...], w_ref[...], preferred_element_type=F32)
    _cast_blocks(rest[:len(rest) // 2], rest[len(rest) // 2 + 1:])


def _out_proj(y, w, res, name, casts=(), tm_target=1056, tn_target=512):
    M, K = y.shape
    N = w.shape[1]
    tm = _pick_tile(M, tm_target, V7X_BF16_ROWS)
    tn = _pick_tile(N, tn_target, V7X_LANES)
    grid = (M // tm, N // tn)
    c_in, c_out, c_shape, c_bytes = _cast_specs(casts, grid)
    blk = _nbytes((tm, K), BF16) + _nbytes((K, tn), BF16) + 3 * _nbytes((tm, tn), F32) + c_bytes
    tile = pl.BlockSpec((tm, tn), lambda i, j: (i, j))
    in_specs = [pl.BlockSpec((tm, K), lambda i, j: (i, 0)), pl.BlockSpec((K, tn), lambda i, j: (0, j)), tile]
    if len(res) == 1:
        body, args = _mm_res_body, (y, w, res[0])
    else:
        z, mu, rs, gamma, beta = res
        stat = pl.BlockSpec((tm, V7X_LANES), lambda i, j: (i, 0))
        vec = pl.BlockSpec((1, tn), lambda i, j: (0, j))
        in_specs += [stat, stat, vec, vec]
        body, args = _mm_lnres_body, (y, w, z, mu, rs, gamma.reshape(1, N), beta.reshape(1, N))
    outs = pl.pallas_call(
        body,
        grid=grid,
        in_specs=in_specs + c_in,
        out_specs=[tile] + c_out,
        out_shape=[jax.ShapeDtypeStruct((M, N), F32)] + c_shape,
        compiler_params=_params(("parallel", "arbitrary"), blk),
        name=name,
    )(*args, *casts)
    return outs[0], tuple(outs[1:])


LN_ROWS = 16
LN_UNROLL = 4


def _row_sum(x):
    w = x.shape[1]
    while w > V7X_LANES and w % (2 * V7X_LANES) == 0:
        w //= 2
        x = x[:, :w] + x[:, w:]
    return jnp.sum(x, axis=1, keepdims=True)


def _ln_rows(r):
    return pl.ds(pl.multiple_of(r * LN_ROWS, LN_ROWS), LN_ROWS)


def _ln_stats(load, n, mu_ref, rs_ref, d_model):
    inv_d = 1.0 / d_model
    stat = (LN_ROWS, V7X_LANES)

    def mean_pass(r, carry):
        mu_ref[_ln_rows(r), :] = jnp.broadcast_to(_row_sum(load(r)) * inv_d, stat)
        return carry

    def var_pass(r, carry):
        d = load(r) - mu_ref[_ln_rows(r), 0:1]
        rs_ref[_ln_rows(r), :] = jnp.broadcast_to(lax.rsqrt(_row_sum(d * d) * inv_d + LN_EPS), stat)
        return carry

    lax.fori_loop(0, n, mean_pass, 0, unroll=LN_UNROLL)
    lax.fori_loop(0, n, var_pass, 0, unroll=LN_UNROLL)


def _ln_apply(zr, r, mu_ref, rs_ref, g_ref, b_ref):
    return (zr - mu_ref[_ln_rows(r), 0:1]) * rs_ref[_ln_rows(r), 0:1] * g_ref[...] + b_ref[...]


def _ln_body(z_ref, g_ref, b_ref, ohb_ref, mu_ref, rs_ref):
    tr, D = z_ref.shape
    n = tr // LN_ROWS

    def load(r):
        return z_ref[_ln_rows(r), :]

    _ln_stats(load, n, mu_ref, rs_ref, D)

    def norm_pass(r, carry):
        ohb_ref[_ln_rows(r), :] = _ln_apply(load(r), r, mu_ref, rs_ref, g_ref, b_ref).astype(BF16)
        return carry

    lax.fori_loop(0, n, norm_pass, 0, unroll=2)


def _layer_norm(z, gamma, beta, name, tr_target=256):
    M, D = z.shape
    tr = _pick_tile(M, tr_target, LN_ROWS * LN_UNROLL)
    blk = _nbytes((tr, D), F32) + _nbytes((tr, D), BF16)
    stat = pl.BlockSpec((tr, V7X_LANES), lambda i: (i, 0))
    return pl.pallas_call(
        _ln_body,
        grid=(M // tr,),
        in_specs=[pl.BlockSpec((tr, D), lambda i: (i, 0)),
                  pl.BlockSpec((1, D), lambda i: (0, 0)),
                  pl.BlockSpec((1, D), lambda i: (0, 0))],
        out_specs=[pl.BlockSpec((tr, D), lambda i: (i, 0)), stat, stat],
        out_shape=[jax.ShapeDtypeStruct((M, D), BF16),
                   jax.ShapeDtypeStruct((M, V7X_LANES), F32),
                   jax.ShapeDtypeStruct((M, V7X_LANES), F32)],
        compiler_params=_params(("parallel",), blk),
        name=name,
    )(z, gamma.reshape(1, D), beta.reshape(1, D))


def _ln_final_body(za_ref, zb_ref, g_ref, b_ref, o_ref, mu_ref, rs_ref, *, shift):
    tr, D = o_ref.shape[1], o_ref.shape[2]
    n = tr // LN_ROWS

    def load(r):
        return za_ref[0, _ln_rows(r), :]

    _ln_stats(load, n, mu_ref, rs_ref, D)

    def norm_pass(r, carry):
        o_ref[0, _ln_rows(r - shift), :] = _ln_apply(load(r), r, mu_ref, rs_ref, g_ref, b_ref)
        return carry

    lax.fori_loop(shift, n, norm_pass, 0)
    for c in range(shift):
        zr = zb_ref[0, c * LN_ROWS:(c + 1) * LN_ROWS, :]
        mu = _row_sum(zr) * (1.0 / D)
        d = zr - mu
        rs = lax.rsqrt(_row_sum(d * d) * (1.0 / D) + LN_EPS)
        o_ref[0, (n - shift + c) * LN_ROWS:(n - shift + c + 1) * LN_ROWS, :] = d * rs * g_ref[...] + b_ref[...]


def _layer_norm_final(z, gamma, beta, B, T_pad, n_meta, S, name, tr_target=256):
    D = z.shape[1]
    assert n_meta % LN_ROWS == 0
    tr = _pick_tile(S, tr_target, LN_ROWS * LN_UNROLL)
    assert tr % n_meta == 0
    per = tr // n_meta
    blk = 2 * _nbytes((tr, D), F32) + _nbytes((n_meta, D), F32)
    z3 = z.reshape(B, T_pad, D)
    return pl.pallas_call(
        functools.partial(_ln_final_body, shift=n_meta // LN_ROWS),
        grid=(B, S // tr),
        in_specs=[pl.BlockSpec((1, tr, D), lambda b, t: (b, t, 0)),
                  pl.BlockSpec((1, n_meta, D), lambda b, t: (b, (t + 1) * per, 0)),
                  pl.BlockSpec((1, D), lambda b, t: (0, 0)),
                  pl.BlockSpec((1, D), lambda b, t: (0, 0))],
        out_specs=pl.BlockSpec((1, tr, D), lambda b, t: (b, t, 0)),
        out_shape=jax.ShapeDtypeStruct((B, S, D), F32),
        scratch_shapes=[pltpu.VMEM((tr, V7X_LANES), F32), pltpu.VMEM((tr, V7X_LANES), F32)],
        compiler_params=_params(("parallel", "parallel"), blk),
        name=name,
    )(z3, z3, gamma.reshape(1, D), beta.reshape(1, D))


SEG = V7X_SUBLANES
GROUP = SEG * SEG


def _sub_shift(v, d, fill, sub):
    return jnp.where(sub >= d, pltpu.roll(v, d, 0), fill)


def _rglru_body(*refs, conv_w, hpb, blk, nl):
    u_refs = refs[:nl]
    g_ref, cw_ref, cb_ref, wg_ref, bg_ref, lam_ref, y_ref = refs[nl:nl + 7]
    ucb, sa, sx, hs, tail, hcar = refs[nl + 7:]
    t = pl.program_id(2)
    tt = g_ref.shape[0]
    ngroup = tt // GROUP
    sub = lax.broadcasted_iota(jnp.int32, (SEG, V7X_LANES), 0)
    tile = (SEG, V7X_LANES)

    @pl.when(t == 0)
    def _():
        tail[...] = jnp.zeros_like(tail)
        hcar[...] = jnp.zeros_like(hcar)

    for c in range(nl):
        lanes = slice(c * V7X_LANES, (c + 1) * V7X_LANES)
        wk = [jnp.broadcast_to(cw_ref[k:k + 1, lanes], tile) for k in range(conv_w)]
        cb = jnp.broadcast_to(cb_ref[:, lanes], tile)
        prev = [tail[c, m] for m in range(conv_w - 1)]
        for gi in range(ngroup):
            cur = [u_refs[c][pl.ds(gi * GROUP + j, SEG, stride=SEG), :] for j in range(SEG)]
            ext = [jnp.where(sub == 0, pltpu.roll(prev[m], 1, 0), pltpu.roll(cur[SEG - (conv_w - 1) + m], 1, 0))
                   for m in range(conv_w - 1)]
            seq = ext + cur
            for j in range(SEG):
                acc = cb
                for k in range(conv_w):
                    acc = acc + wk[k] * seq[j + k]
                ucb[gi * GROUP + j * SEG:gi * GROUP + (j + 1) * SEG, lanes] = acc
            prev = cur[SEG - (conv_w - 1):]
        for m in range(conv_w - 1):
            tail[c, m] = prev[m]

    lam = lam_ref[...]
    logsig = -(jnp.maximum(-lam, 0.0) + jnp.log1p(jnp.exp(-jnp.abs(lam))))
    for hh in range(hpb):
        cols = slice(hh * blk, (hh + 1) * blk)
        ucs = ucb[:, cols]
        gate = jnp.dot(ucs.astype(BF16), wg_ref[hh], preferred_element_type=F32) + bg_ref[hh]
        r = _sigmoid(gate[:, :blk])
        i = _sigmoid(gate[:, blk:])
        a = jnp.exp(LRU_C * r * logsig[:, cols])
        sa[:, cols] = a
        sx[:, cols] = (ucs * i) * jnp.sqrt(1.0 - a * a)

    for c in range(nl):
        lanes = slice(c * V7X_LANES, (c + 1) * V7X_LANES)
        h_in = hcar[c]
        for gi in range(ngroup):
            hloc, ploc = [], []
            for j in range(SEG):
                rows = slice(gi * GROUP + j * SEG, gi * GROUP + (j + 1) * SEG)
                a = sa[rows, lanes]
                x = sx[rows, lanes]
                hloc.append(x if j == 0 else a * hloc[-1] + x)
                ploc.append(a if j == 0 else a * ploc[-1])
            at, xt = ploc[-1], hloc[-1]
            d = 1
            while d < SEG:
                xt = at * _sub_shift(xt, d, 0.0, sub) + xt
                at = at * _sub_shift(at, d, 1.0, sub)
                d *= 2
            carry = jnp.where(sub == 0, h_in, pltpu.roll(xt, 1, 0) + pltpu.roll(at, 1, 0) * h_in)
            for j in range(SEG):
                h = hloc[j] + ploc[j] * carry
                hs[c, pl.ds(gi * GROUP + j, SEG, stride=SEG), :] = h
            h_in = jnp.broadcast_to(h[SEG - 1:SEG, :], tile)
        hcar[c] = h_in

    for c in range(nl):
        lanes = slice(c * V7X_LANES, (c + 1) * V7X_LANES)
        y_ref[:, lanes] = (hs[c] * _silu(g_ref[:, lanes])).astype(BF16)


def _rglru_mix(ug, conv_w, conv_b, w_a, b_a, w_x, b_x, lam, B, T_pad):
    BR = conv_b.shape[0]
    heads, blk, _ = w_a.shape
    K = conv_w.shape[0]
    hpb = 2 if heads % 2 == 0 else 1
    cw = hpb * blk
    nl = cw // V7X_LANES
    ncb = BR // cw
    tt = _pick_tile(T_pad, 384, GROUP)
    nT = T_pad // tt
    wg = jnp.concatenate([w_a, w_x], axis=-1).astype(BF16)
    bg = jnp.concatenate([b_a.reshape(heads, 1, blk), b_x.reshape(heads, 1, blk)], axis=-1)
    body = functools.partial(_rglru_body, conv_w=K, hpb=hpb, blk=blk, nl=nl)
    blkbytes = 7 * _nbytes((tt, cw), F32) + _nbytes((hpb, blk, 2 * blk), BF16)

    def u_spec(l):
        return pl.BlockSpec((tt, V7X_LANES), lambda b, c, t: (b * nT + t, c * nl + l))

    return pl.pallas_call(
        body,
        grid=(B, ncb, nT),
        in_specs=[u_spec(l) for l in range(nl)] + [
            pl.BlockSpec((tt, cw), lambda b, c, t: (b * nT + t, c + ncb)),
            pl.BlockSpec((K, cw), lambda b, c, t: (0, c)),
            pl.BlockSpec((1, cw), lambda b, c, t: (0, c)),
            pl.BlockSpec((hpb, blk, 2 * blk), lambda b, c, t: (c, 0, 0)),
            pl.BlockSpec((hpb, 1, 2 * blk), lambda b, c, t: (c, 0, 0)),
            pl.BlockSpec((1, cw), lambda b, c, t: (0, c))],
        out_specs=pl.BlockSpec((tt, cw), lambda b, c, t: (b * nT + t, c)),
        out_shape=jax.ShapeDtypeStruct((B * T_pad, BR), BF16),
        scratch_shapes=[pltpu.VMEM((tt, cw), F32),
                        pltpu.VMEM((tt, cw), F32),
                        pltpu.VMEM((tt, cw), F32),
                        pltpu.VMEM((nl, tt, V7X_LANES), F32),
                        pltpu.VMEM((nl, K - 1, SEG, V7X_LANES), F32),
                        pltpu.VMEM((nl, SEG, V7X_LANES), F32)],
        compiler_params=_params(("parallel", "parallel", "arbitrary"), blkbytes),
        name="rglru_mix",
    )(*([ug] * nl), ug, conv_w.reshape(K, BR), conv_b.reshape(1, BR), wg, bg, lam.reshape(1, BR))


POOL_HIST = 24


def _pool_body(u_ref, g_ref, w_ref, sc_ref, y_ref, ubuf, sbuf, pbuf, *, windows):
    grp = pl.program_id(0)
    t = pl.program_id(2)
    tt = u_ref.shape[1]
    gw = u_ref.shape[2]
    n = POOL_HIST + tt

    @pl.when(t == 0)
    def _():
        ubuf[0:POOL_HIST, :] = jnp.zeros((POOL_HIST, gw), F32)

    sbuf[0:V7X_SUBLANES, :] = jnp.zeros((V7X_SUBLANES, gw), F32)
    ubuf[POOL_HIST:n, :] = u_ref[0]
    pos = lax.broadcasted_iota(jnp.int32, (tt, 1), 0) + t * tt + 1

    for gi, w in enumerate(windows):
        @pl.when(grp == gi)
        def _(w=w):
            lo = V7X_SUBLANES
            s = ubuf[lo:n, :] + ubuf[lo - 1:n - 1, :]
            d = 2
            while d < w:
                sbuf[lo:n, :] = s
                s = sbuf[lo:n, :] + sbuf[lo - d:n - d, :]
                d *= 2
            wsum = s[POOL_HIST - lo:, :]
            cnt = jnp.minimum(pos, w).astype(F32)
            pbuf[...] = (wsum / cnt - u_ref[0]).astype(BF16)

    ubuf[0:POOL_HIST, :] = ubuf[tt:n, :]
    mixed = jnp.dot(pbuf[...], w_ref[0], preferred_element_type=F32) * sc_ref[...]
    y_ref[0] = (mixed * _silu(g_ref[0])).astype(BF16)


def _pool_mix(ug, w_grp, scale, B, T_pad):
    ngrp, gw, _ = w_grp.shape
    BR = ngrp * gw
    tt = _pick_tile(T_pad, 384, V7X_BF16_ROWS)
    ug3 = ug.reshape(B, T_pad, 2 * BR)
    body = functools.partial(_pool_body, windows=POOL_WINDOWS[:ngrp])
    blkbytes = 5 * _nbytes((tt, gw), F32) + _nbytes((gw, gw), BF16)
    return pl.pallas_call(
        body,
        grid=(ngrp, B, T_pad // tt),
        in_specs=[pl.BlockSpec((1, tt, gw), lambda g, b, t: (b, t, g)),
                  pl.BlockSpec((1, tt, gw), lambda g, b, t: (b, t, g + ngrp)),
                  pl.BlockSpec((1, gw, gw), lambda g, b, t: (g, 0, 0)),
                  pl.BlockSpec((1, gw), lambda g, b, t: (0, g))],
        out_specs=pl.BlockSpec((1, tt, gw), lambda g, b, t: (b, t, g)),
        out_shape=jax.ShapeDtypeStruct((B, T_pad, BR), BF16),
        scratch_shapes=[pltpu.VMEM((POOL_HIST + tt, gw), F32),
                        pltpu.VMEM((POOL_HIST + tt, gw), F32),
                        pltpu.VMEM((tt, gw), BF16)],
        compiler_params=_params(("parallel", "parallel", "arbitrary"), blkbytes),
        name="pool_mix",
    )(ug3, ug3, w_grp.astype(BF16), scale.reshape(1, BR))


def _rope_pairs(x, c_ref, s_ref):
    half = MLA_ROPE // 2
    lane = lax.broadcasted_iota(jnp.int32, x.shape, 1)
    partner = jnp.where(lane % MLA_ROPE < half,
                        pltpu.roll(x, V7X_LANES - half, 1), pltpu.roll(x, half, 1))
    return x * c_ref[...] + partner * s_ref[...]


def _rms_scale(x, g):
    return x * lax.rsqrt(jnp.mean(jnp.square(x), axis=-1, keepdims=True) + RMS_EPS) * g


def _qproj_body(c_ref, n_ref, w_ref, cos_ref, sin_ref, q_ref, xn_ref, *, ppb):
    @pl.when(pl.program_id(1) == 0)
    def _():
        xn_ref[...] = _rms_scale(c_ref[...], n_ref[...]).astype(BF16)

    res = jnp.dot(xn_ref[...], w_ref[...], preferred_element_type=F32)
    nn = 2 * MLA_NOPE
    pw = nn + 2 * MLA_ROPE
    for p in range(ppb):
        q_ref[:, p * pw:p * pw + nn] = res[:, p * pw:p * pw + nn].astype(BF16)
        q_ref[:, p * pw + nn:(p + 1) * pw] = _rope_pairs(
            res[:, p * pw + nn:(p + 1) * pw], cos_ref, sin_ref).astype(BF16)


def _kvproj_body(c_ref, kr_ref, n_ref, w_ref, cos_ref, sin_ref, kv_ref, krd_ref, xn_ref):
    @pl.when(pl.program_id(1) == 0)
    def _():
        xn_ref[...] = _rms_scale(c_ref[...], n_ref[...]).astype(BF16)
        rot = _rope_pairs(kr_ref[...], cos_ref, sin_ref)
        lane = lax.broadcasted_iota(jnp.int32, rot.shape, 1)
        krd_ref[...] = jnp.where(lane < MLA_ROPE, rot, pltpu.roll(rot, MLA_ROPE, 1)).astype(BF16)

    kv_ref[...] = jnp.dot(xn_ref[...], w_ref[...], preferred_element_type=F32).astype(BF16)


def _attn_body(q_ref, kn_ref, v_ref, kr_ref, g_ref, y_ref, m_ref, acc_ref, *, scale2, pps):
    qi = pl.program_id(2)
    tq = q_ref.shape[1]
    wide = 2 * tq
    nn = 2 * MLA_NOPE
    pw = nn + 2 * MLA_ROPE
    heads = [(pp, hh) for pp in range(pps) for hh in range(2)]
    lane = lax.broadcasted_iota(jnp.int32, (tq, V7X_LANES), 1)
    qs = []
    for pp, hh in heads:
        qr = q_ref[0, :, pp * pw + nn:(pp + 1) * pw]
        mine = (lane < MLA_ROPE) if hh == 0 else (lane >= MLA_ROPE)
        nope = q_ref[0, :, pp * pw + hh * MLA_NOPE:pp * pw + (hh + 1) * MLA_NOPE]
        qs.append(jnp.concatenate([nope, jnp.where(mine, qr, jnp.zeros_like(qr))], axis=1))

    m_ref[...] = jnp.full_like(m_ref, -jnp.inf)
    acc_ref[...] = jnp.zeros_like(acc_ref)

    def block(k0, tk, mask_shift):
        krope = kr_ref[0, pl.ds(k0, tk), :]
        ones = jnp.ones((tk, V7X_LANES), BF16)
        nc = tk // V7X_LANES
        for h, (pp, hh) in enumerate(heads):
            hc = slice((2 * pp + hh) * MLA_NOPE, (2 * pp + hh + 1) * MLA_NOPE)
            kf = jnp.concatenate([kn_ref[0, pl.ds(k0, tk), hc], krope], axis=1)
            s = lax.dot_general(qs[h], kf, (((1,), (1,)), ((), ())),
                                preferred_element_type=F32) * scale2
            if mask_shift is not None:
                rq = lax.broadcasted_iota(jnp.int32, (tq, tk), 0)
                ck = lax.broadcasted_iota(jnp.int32, (tq, tk), 1)
                s = jnp.where(ck <= rq + mask_shift, s, -1e30)
            cols = [s[:, c * V7X_LANES:(c + 1) * V7X_LANES] for c in range(nc)]
            mloc = cols[0]
            for c in range(1, nc):
                mloc = jnp.maximum(mloc, cols[c])
            m_old = m_ref[h]
            m_new = jnp.maximum(m_old, jnp.max(mloc, axis=1, keepdims=True))
            p = jnp.concatenate([jnp.exp2(cols[c] - m_new) for c in range(nc)], axis=1).astype(BF16)
            corr = jnp.exp2(m_old - m_new)
            vf = jnp.concatenate([v_ref[0, pl.ds(k0, tk), hc], ones], axis=1)
            pv = jnp.dot(p, vf, preferred_element_type=F32)
            acc_ref[h] = jnp.concatenate([corr, corr], axis=1) * acc_ref[h] + pv
            m_ref[h] = m_new

    nblk = qi // 2

    def loop_body(j, carry):
        block(pl.multiple_of(2 * j * wide, wide), wide, None)
        block(pl.multiple_of((2 * j + 1) * wide, wide), wide, None)
        return carry

    lax.fori_loop(0, nblk // 2, loop_body, 0)

    @pl.when(nblk % 2 == 1)
    def _():
        block(pl.multiple_of((nblk - 1) * wide, wide), wide, None)

    @pl.when(qi % 2 == 1)
    def _():
        block(pl.multiple_of((qi - 1) * tq, tq), wide, tq)

    @pl.when(qi % 2 == 0)
    def _():
        block(pl.multiple_of(qi * tq, tq), tq, 0)

    gate = _silu(g_ref[0])
    for h, (pp, hh) in enumerate(heads):
        hc = slice((2 * pp + hh) * MLA_NOPE, (2 * pp + hh + 1) * MLA_NOPE)
        acc = acc_ref[h]
        o = acc[:, :MLA_NOPE] / acc[:, MLA_NOPE:]
        y_ref[0, :, hc] = (o * gate[:, hc]).astype(BF16)


def _mla_mix(hb, w_in_p, q_norm, w_uq, kv_norm, w_ukv, B, T_pad):
    q_lora = q_norm.shape[0]
    kv_lora = kv_norm.shape[0]
    heads = w_uq.shape[1] // (MLA_NOPE + MLA_ROPE)
    vdim = w_ukv.shape[1] // heads - MLA_NOPE
    assert vdim == MLA_NOPE and heads % 2 == 0
    BR = heads * vdim
    npair = heads // 2
    M = hb.shape[0]
    D = hb.shape[1]

    c_all = _matmul(hb, w_in_p, F32, "mla_in", tn_target=1152)
    n_all = w_in_p.shape[1]

    tm = _pick_tile(T_pad, 1056, V7X_BF16_ROWS)
    nt = T_pad // tm

    half = MLA_ROPE // 2
    inv = ROPE_BASE ** (-jnp.arange(0, MLA_ROPE, 2, dtype=F32) / MLA_ROPE)
    ang = jnp.arange(T_pad, dtype=F32)[:, None] * inv[None, :]
    cos_t = jnp.tile(jnp.cos(ang), (1, 4))
    sin_t = jnp.tile(jnp.concatenate([-jnp.sin(ang), jnp.sin(ang)], axis=1), (1, 2))
    del half

    hd = MLA_NOPE + MLA_ROPE
    wq = w_uq.reshape(q_lora, npair, 2, hd)
    wq = jnp.concatenate([wq[..., 0, :MLA_NOPE], wq[..., 1, :MLA_NOPE],
                          wq[..., 0, MLA_NOPE:], wq[..., 1, MLA_NOPE:]], axis=-1)
    pw = 2 * hd
    wq = wq.reshape(q_lora, npair * pw).astype(BF16)
    cq_blk = BR // q_lora
    ppb = _pick_tile(npair, 4, 1)
    qw = ppb * pw
    q = pl.pallas_call(
        functools.partial(_qproj_body, ppb=ppb),
        grid=(M // tm, npair // ppb),
        in_specs=[pl.BlockSpec((tm, q_lora), lambda i, j: (i, cq_blk)),
                  pl.BlockSpec((1, q_lora), lambda i, j: (0, 0)),
                  pl.BlockSpec((q_lora, qw), lambda i, j: (0, j)),
                  pl.BlockSpec((tm, V7X_LANES), lambda i, j: (i % nt, 0)),
                  pl.BlockSpec((tm, V7X_LANES), lambda i, j: (i % nt, 0))],
        out_specs=pl.BlockSpec((tm, qw), lambda i, j: (i, j)),
        out_shape=jax.ShapeDtypeStruct((M, npair * pw), BF16),
        scratch_shapes=[pltpu.VMEM((tm, q_lora), BF16)],
        compiler_params=_params(("parallel", "arbitrary"),
                                _nbytes((tm, q_lora), F32) + _nbytes((q_lora, qw), BF16)
                                + 2 * _nbytes((tm, qw), F32)),
        name="mla_qproj",
    )(c_all, q_norm.reshape(1, q_lora), wq, cos_t, sin_t)

    wkv = w_ukv.reshape(kv_lora, heads, MLA_NOPE + vdim)
    wkv = jnp.concatenate([wkv[..., :MLA_NOPE].reshape(kv_lora, BR),
                           wkv[..., MLA_NOPE:].reshape(kv_lora, BR)], axis=1).astype(BF16)
    tn = _pick_tile(2 * BR, 1024, V7X_LANES)
    ckv_blk = (BR + q_lora) // kv_lora
    kr_blk = (BR + q_lora + kv_lora) // V7X_LANES
    kv, krd = pl.pallas_call(
        _kvproj_body,
        grid=(M // tm, 2 * BR // tn),
        in_specs=[pl.BlockSpec((tm, kv_lora), lambda i, j: (i, ckv_blk)),
                  pl.BlockSpec((tm, V7X_LANES), lambda i, j: (i, kr_blk)),
                  pl.BlockSpec((1, kv_lora), lambda i, j: (0, 0)),
                  pl.BlockSpec((kv_lora, tn), lambda i, j: (0, j)),
                  pl.BlockSpec((tm, V7X_LANES), lambda i, j: (i % nt, 0)),
                  pl.BlockSpec((tm, V7X_LANES), lambda i, j: (i % nt, 0))],
        out_specs=[pl.BlockSpec((tm, tn), lambda i, j: (i, j)),
                   pl.BlockSpec((tm, V7X_LANES), lambda i, j: (i, 0))],
        out_shape=[jax.ShapeDtypeStruct((M, 2 * BR), BF16),
                   jax.ShapeDtypeStruct((M, V7X_LANES), BF16)],
        scratch_shapes=[pltpu.VMEM((tm, kv_lora), BF16)],
        compiler_params=_params(("parallel", "arbitrary"),
                                _nbytes((tm, kv_lora), F32) + _nbytes((kv_lora, tn), BF16)
                                + 2 * _nbytes((tm, tn), F32)),
        name="mla_kvproj",
    )(c_all, c_all, kv_norm.reshape(1, kv_lora), wkv, cos_t, sin_t)

    tq = _pick_tile(T_pad, 384, V7X_BF16_ROWS)
    pps = _pick_tile(npair, 4, 1)
    ngrp = npair // pps
    nh = 2 * pps
    pv = nh * vdim
    qw = pps * pw
    q3 = q.reshape(B, T_pad, npair * pw)
    kv3 = kv.reshape(B, T_pad, 2 * BR)
    krd3 = krd.reshape(B, T_pad, V7X_LANES)
    c3 = c_all.reshape(B, T_pad, n_all)
    body = functools.partial(_attn_body, scale2=float(hd) ** -0.5 * LOG2_E, pps=pps)
    blkbytes = (_nbytes((tq, qw), BF16) + 2 * _nbytes((T_pad, pv), BF16) + _nbytes((T_pad, V7X_LANES), BF16)
                + _nbytes((tq, pv), F32) + _nbytes((tq, pv), BF16) + 2 * nh * _nbytes((tq, 2 * tq), F32))
    return pl.pallas_call(
        body,
        grid=(B, ngrp, T_pad // tq),
        in_specs=[pl.BlockSpec((1, tq, qw), lambda b, p, i: (b, i, p)),
                  pl.BlockSpec((1, T_pad, pv), lambda b, p, i: (b, 0, p)),
                  pl.BlockSpec((1, T_pad, pv), lambda b, p, i: (b, 0, p + ngrp)),
                  pl.BlockSpec((1, T_pad, V7X_LANES), lambda b, p, i: (b, 0, 0)),
                  pl.BlockSpec((1, tq, pv), lambda b, p, i: (b, i, p))],
        out_specs=pl.BlockSpec((1, tq, pv), lambda b, p, i: (b, i, p)),
        out_shape=jax.ShapeDtypeStruct((B, T_pad, BR), BF16),
        scratch_shapes=[pltpu.VMEM((nh, tq, V7X_LANES), F32),
                        pltpu.VMEM((nh, tq, 2 * vdim), F32)],
        compiler_params=_params(("parallel", "parallel", "arbitrary"), blkbytes),
        name="mla_attn",
    )(q3, kv3, kv3, krd3, c3).reshape(M, BR)


def _mm_rope_body(a_ref, w_ref, cos_ref, sin_ref, o_ref, *, dk, scale):
    res = jnp.dot(a_ref[...], w_ref[...], preferred_element_type=F32)
    cos = cos_ref[...]
    sin = sin_ref[...]
    hd = dk // 2
    for h in range(o_ref.shape[1] // dk):
        x1 = res[:, h * dk:h * dk + hd]
        x2 = res[:, h * dk + hd:(h + 1) * dk]
        o_ref[:, h * dk:h * dk + hd] = ((x1 * cos - x2 * sin) * scale).astype(o_ref.dtype)
        o_ref[:, h * dk + hd:(h + 1) * dk] = ((x2 * cos + x1 * sin) * scale).astype(o_ref.dtype)


def _matmul_rope(a, w, cols, cos_t, sin_t, dk, scale, out_dtype, name, tm_target=1056, tn_target=1024):
    M, K = a.shape
    start, N = cols
    T_pad = cos_t.shape[0]
    tm = _pick_tile(T_pad, tm_target, V7X_BF16_ROWS)
    tn = _pick_tile(N, tn_target, dk)
    assert start % tn == 0
    j0 = start // tn
    nt = T_pad // tm
    blk = (_nbytes((tm, K), a.dtype) + _nbytes((K, tn), w.dtype) + 2 * _nbytes((tm, tn), F32)
           + 2 * _nbytes((tm, dk // 2), F32))
    return pl.pallas_call(
        functools.partial(_mm_rope_body, dk=dk, scale=scale),
        grid=(M // tm, N // tn),
        in_specs=[pl.BlockSpec((tm, K), lambda i, j: (i, 0)),
                  pl.BlockSpec((K, tn), lambda i, j: (0, j + j0)),
                  pl.BlockSpec((tm, dk // 2), lambda i, j: (i % nt, 0)),
                  pl.BlockSpec((tm, dk // 2), lambda i, j: (i % nt, 0))],
        out_specs=pl.BlockSpec((tm, tn), lambda i, j: (i, j)),
        out_shape=jax.ShapeDtypeStruct((M, N), out_dtype),
        compiler_params=_params(("parallel", "arbitrary"), blk),
        name=name,
    )(a, w, cos_t, sin_t)


def _ret_body(lg_ref, q_ref, k_ref, v_ref, g_ref, y_ref, st_ref, dec_ref, *, hps, dk):
    c = pl.program_id(2)
    C = q_ref.shape[1]

    @pl.when(c == 0)
    def _():
        st_ref[...] = jnp.zeros_like(st_ref)
        ri = lax.broadcasted_iota(jnp.int32, (C, C), 0)
        ci = lax.broadcasted_iota(jnp.int32, (C, C), 1)
        diff = (ri - ci).astype(F32)
        for h in range(hps):
            dec_ref[h] = jnp.where(diff >= 0, jnp.exp(jnp.maximum(diff, 0.0) * lg_ref[h, :, 0:1]), 0.0)

    jcol = lax.broadcasted_iota(jnp.int32, (C, 1), 0).astype(F32)
    for h in range(hps):
        cols = slice(h * dk, (h + 1) * dk)
        lg = lg_ref[h, :, 0:1]
        qb = q_ref[0, :, cols]
        kf = k_ref[0, :, cols]
        vb = v_ref[0, :, cols]
        s = lax.dot_general(qb, kf.astype(BF16), (((1,), (1,)), ((), ())), preferred_element_type=F32)
        intra = jnp.dot((s * dec_ref[h]).astype(BF16), vb, preferred_element_type=F32)
        state = st_ref[h]
        inter = jnp.dot(qb, state.astype(BF16), preferred_element_type=F32) * jnp.exp((jcol + 1.0) * lg)
        kw = (kf * jnp.exp((C - 1.0 - jcol) * lg)).astype(BF16)
        st_ref[h] = jnp.exp(C * lg) * state + lax.dot_general(
            kw, vb, (((0,), (0,)), ((), ())), preferred_element_type=F32)
        o = intra + inter
        mu = jnp.mean(o, axis=-1, keepdims=True)
        var = jnp.mean(jnp.square(o - mu), axis=-1, keepdims=True)
        o = (o - mu) * lax.rsqrt(var + LN_EPS)
        y_ref[0, :, cols] = (o * _silu(g_ref[0, :, cols])).astype(BF16)


def _ret_mix(hb, wb, heads, B, T_pad):
    BR = wb.shape[1] // 4
    dk = BR // heads
    C = _pick_tile(T_pad, 384, V7X_BF16_ROWS)
    hps = 2 if heads % 2 == 0 else 1
    inv = ROPE_BASE ** (-jnp.arange(0, dk, 2, dtype=F32) / dk)
    ang = jnp.arange(T_pad, dtype=F32)[:, None] * inv[None, :]
    cos_t = jnp.cos(ang)
    sin_t = jnp.sin(ang)
    q =_matmul_rope(hb, wb, (0, BR), cos_t, sin_t, dk, 1.0, BF16, "l3_q")
    k = _matmul_rope(hb, wb, (BR, BR), cos_t, sin_t, dk, float(dk) ** -0.5, F32, "l3_k")
    v = _matmul(hb, wb, BF16, "l3_v", cols=(2 * BR, BR))
    g = _matmul(hb, wb, F32, "l3_g", cols=(3 * BR, BR))

    log_g = jnp.log(1.0 - 2.0 ** (-5.0 - jnp.arange(heads, dtype=F32)))
    lg = jnp.broadcast_to(log_g[:, None, None], (heads, 1, V7X_LANES))
    cw = hps * dk
    spec = pl.BlockSpec((1, C, cw), lambda b, h, c: (b, c, h))
    blkbytes = (2 * _nbytes((C, cw), F32) + 3 * _nbytes((C, cw), BF16)
                + hps * (4 * _nbytes((C, C), F32) + 2 * _nbytes((dk, dk), F32)))
    return pl.pallas_call(
        functools.partial(_ret_body, hps=hps, dk=dk),
        grid=(B, heads // hps, T_pad // C),
        in_specs=[pl.BlockSpec((hps, 1, V7X_LANES), lambda b, h, c: (h, 0, 0)), spec, spec, spec, spec],
        out_specs=spec,
        out_shape=jax.ShapeDtypeStruct((B, T_pad, BR), BF16),
        scratch_shapes=[pltpu.VMEM((hps, dk, dk), F32), pltpu.VMEM((hps, C, C), F32)],
        compiler_params=_params(("parallel", "parallel", "arbitrary"), blkbytes),
        name="ret_mix",
    )(lg, q.reshape(B, T_pad, BR), k.reshape(B, T_pad, BR), v.reshape(B, T_pad, BR),
      g.reshape(B, T_pad, BR)).reshape(B * T_pad, BR)


RET_HEADS = 16


def kernel(x, meta_tokens, l0_w_in, l0_conv_w, l0_conv_b, l0_w_a, l0_b_a, l0_w_x, l0_b_x, l0_lam, l0_w_out, l0_ln_g, l0_ln_b, l1_w_in, l1_w_grp, l1_scale, l1_w_out, l1_ln_g, l1_ln_b, l2_w_in, l2_q_norm, l2_w_uq, l2_kv_norm, l2_w_ukv, l2_w_out, l2_ln_g, l2_ln_b, l3_w_in, l3_w_out, l3_ln_g, l3_ln_b):
    B, S, D = x.shape
    n_meta = meta_tokens.shape[0]
    T = n_meta + S
    T_pad = -(-T // TIME_ALIGN) * TIME_ALIGN
    M = B * T_pad

    meta = jnp.broadcast_to(meta_tokens.astype(x.dtype)[None], (B, n_meta, D))
    h = jnp.concatenate([meta, x, jnp.zeros((B, T_pad - T, D), x.dtype)], axis=1).reshape(M, D)
    hb = h.astype(BF16)

    l2_w_in_p = jnp.pad(l2_w_in, ((0, 0), (0, -l2_w_in.shape[1] % V7X_LANES)))
    ug, (w0_out, w1_in) = _matmul(hb, l0_w_in.astype(BF16), F32, "l0_in", casts=(l0_w_out, l1_w_in))
    y = _rglru_mix(ug, l0_conv_w, l0_conv_b, l0_w_a, l0_b_a, l0_w_x, l0_b_x, l0_lam, B, T_pad)
    z, (w1_out,) = _out_proj(y.reshape(M, -1), w0_out, (h,), "l0_out_mm", casts=(l1_w_out,))
    hb, mu, rs = _layer_norm(z, l0_ln_g, l0_ln_b, "l0_out_ln")
    res = (z, mu, rs, l0_ln_g, l0_ln_b)

    ug, (w2_in, w2_out) = _matmul(hb, w1_in, F32, "l1_in", casts=(l2_w_in_p, l2_w_out))
    y = _pool_mix(ug, l1_w_grp, l1_scale, B, T_pad)
    z, (w3_in,) = _out_proj(y.reshape(M, -1), w1_out, res, "l1_out_mm", casts=(l3_w_in,))
    hb, mu, rs = _layer_norm(z, l1_ln_g, l1_ln_b, "l1_out_ln")
    res = (z, mu, rs, l1_ln_g, l1_ln_b)

    y = _mla_mix(hb, w2_in, l2_q_norm, l2_w_uq, l2_kv_norm, l2_w_ukv, B, T_pad)
    z, (w3_out,) = _out_proj(y, w2_out, res, "l2_out_mm", casts=(l3_w_out,))
    hb, mu, rs = _layer_norm(z, l2_ln_g, l2_ln_b, "l2_out_ln")
    res = (z, mu, rs, l2_ln_g, l2_ln_b)

    y = _ret_mix(hb, w3_in, RET_HEADS, B, T_pad)
    z, _ = _out_proj(y, w3_out, res, "l3_out_mm")
    return _layer_norm_final(z, l3_ln_g, l3_ln_b, B, T_pad, n_meta, S, "l3_out_ln")
```

```python
import functools

import jax
import jax.numpy as jnp
from jax import lax
from jax.experimental import pallas as pl
from jax.experimental.pallas import tpu as pltpu

F32 = jnp.float32
BF16 = jnp.bfloat16

DEPTH = 4
ALPHA = (2.0 * DEPTH) ** 0.25
LN_EPS = 1e-5
RMS_EPS = 1e-6
ROPE_BASE = 10000.0
LRU_C = 8.0
POOL_WINDOWS = (2, 4, 8, 16)
MLA_NOPE = 128
MLA_ROPE = 64
LOG2_E = 1.4426950408889634

V7X_LANES = 128
V7X_SUBLANES = 8
V7X_BF16_ROWS = 16
V7X_SCOPED_VMEM_MAX = 60000 * 1024
V7X_VMEM_INTERNAL = 12 * 1024 * 1024

TIME_ALIGN = 128


def _pick_tile(n, target, mult):
    best = None
    for d in range(mult, min(n, target) + 1, mult):
        if n % d == 0:
            best = d
    return n if best is None else best


def _params(semantics, block_bytes):
    limit = min(2 * block_bytes + V7X_VMEM_INTERNAL, V7X_SCOPED_VMEM_MAX)
    return pltpu.CompilerParams(dimension_semantics=semantics, vmem_limit_bytes=int(limit))


def _nbytes(shape, dtype):
    n = 1
    for s in shape:
        n *= s
    return n * jnp.dtype(dtype).itemsize


def _sigmoid(x):
    return 1.0 / (1.0 + jnp.exp(-x))


def _silu(x):
    return x * _sigmoid(x)


def _cast_specs(casts, grid):
    ni, nj = grid
    nsteps = ni * nj
    in_specs, out_specs, shapes, nbytes = [], [], [], 0
    for wt in casts:
        R, C = wt.shape
        rows = R // nsteps
        assert R % nsteps == 0 and rows % V7X_BF16_ROWS == 0
        spec = pl.BlockSpec((rows, C), lambda i, j: (i * nj + j, 0))
        in_specs.append(spec)
        out_specs.append(spec)
        shapes.append(jax.ShapeDtypeStruct((R, C), BF16))
        nbytes += _nbytes((rows, C), F32) + _nbytes((rows, C), BF16)
    return in_specs, out_specs, shapes, nbytes


def _cast_blocks(src_refs, dst_refs):
    for src, dst in zip(src_refs, dst_refs):
        dst[...] = src[...].astype(BF16)


def _mm_body(a_ref, w_ref, *rest):
    o_ref = rest[len(rest) // 2]
    o_ref[...] = jnp.dot(a_ref[...], w_ref[...], preferred_element_type=F32).astype(o_ref.dtype)
    _cast_blocks(rest[:len(rest) // 2], rest[len(rest) // 2 + 1:])


def _matmul(a, w, out_dtype, name, cols=None, casts=(), tm_target=1056, tn_target=1024):
    M, K = a.shape
    start, N = (0, w.shape[1]) if cols is None else cols
    tm = _pick_tile(M, tm_target, V7X_BF16_ROWS)
    tn = _pick_tile(N, tn_target, V7X_LANES)
    assert start % tn == 0
    j0 = start // tn
    grid = (M // tm, N // tn)
    c_in, c_out, c_shape, c_bytes = _cast_specs(casts, grid)
    blk = _nbytes((tm, K), a.dtype) + _nbytes((K, tn), w.dtype) + _nbytes((tm, tn), out_dtype) + c_bytes
    outs = pl.pallas_call(
        _mm_body,
        grid=grid,
        in_specs=[pl.BlockSpec((tm, K), lambda i, j: (i, 0)),
                  pl.BlockSpec((K, tn), lambda i, j: (0, j + j0))] + c_in,
        out_specs=[pl.BlockSpec((tm, tn), lambda i, j: (i, j))] + c_out,
        out_shape=[jax.ShapeDtypeStruct((M, N), out_dtype)] + c_shape,
        compiler_params=_params(("parallel", "arbitrary"), blk),
        name=name,
    )(a, w, *casts)
    return (outs[0], tuple(outs[1:])) if casts else outs[0]


def _mm_res_body(y_ref, w_ref, h_ref, *rest):
    o_ref = rest[len(rest) // 2]
    o_ref[...] = ALPHA * h_ref[...] + jnp.dot(y_ref[...], w_ref[...], preferred_element_type=F32)
    _cast_blocks(rest[:len(rest) // 2], rest[len(rest) // 2 + 1:])


def _mm_lnres_body(y_ref, w_ref, z_ref, mu_ref, rs_ref, g_ref, b_ref, *rest):
    o_ref = rest[len(rest) // 2]
    h = (z_ref[...] - mu_ref[:, 0:1]) * rs_ref[:, 0:1] * g_ref[...] + b_ref[...]
    o_ref[...] = ALPHA * h + jnp.dot(y_ref[...], w_ref[...], preferred_element_type=F32)
    _cast_blocks(rest[:len(rest) // 2], rest[len(rest) // 2 + 1:])


def _out_proj(y, w, res, name, casts=(), tm_target=1056, tn_target=512):
    M, K = y.shape
    N = w.shape[1]
    tm = _pick_tile(M, tm_target, V7X_BF16_ROWS)
    tn = _pick_tile(N, tn_target, V7X_LANES)
    grid = (M // tm, N // tn)
    c_in, c_out, c_shape, c_bytes = _cast_specs(casts, grid)
    blk = _nbytes((tm, K), BF16) + _nbytes((K, tn), BF16) + 3 * _nbytes((tm, tn), F32) + c_bytes
    tile = pl.BlockSpec((tm, tn), lambda i, j: (i, j))
    in_specs = [pl.BlockSpec((tm, K), lambda i, j: (i, 0)), pl.BlockSpec((K, tn), lambda i, j: (0, j)), tile]
    if len(res) == 1:
        body, args = _mm_res_body, (y, w, res[0])
    else:
        z, mu, rs, gamma, beta = res
        stat = pl.BlockSpec((tm, V7X_LANES), lambda i, j: (i, 0))
        vec = pl.BlockSpec((1, tn), lambda i, j: (0, j))
        in_specs += [stat, stat, vec, vec]
        body, args = _mm_lnres_body, (y, w, z, mu, rs, gamma.reshape(1, N), beta.reshape(1, N))
    outs = pl.pallas_call(
        body,
        grid=grid,
        in_specs=in_specs + c_in,
        out_specs=[tile] + c_out,
        out_shape=[jax.ShapeDtypeStruct((M, N), F32)] + c_shape,
        compiler_params=_params(("parallel", "arbitrary"), blk),
        name=name,
    )(*args, *casts)
    return outs[0], tuple(outs[1:])


LN_ROWS = 16
LN_UNROLL = 4


def _row_sum(x):
    w = x.shape[1]
    while w > V7X_LANES and w % (2 * V7X_LANES) == 0:
        w //= 2
        x = x[:, :w] + x[:, w:]
    return jnp.sum(x, axis=1, keepdims=True)


def _ln_rows(r):
    return pl.ds(pl.multiple_of(r * LN_ROWS, LN_ROWS), LN_ROWS)


def _ln_stats(load, n, mu_ref, rs_ref, d_model):
    inv_d = 1.0 / d_model
    stat = (LN_ROWS, V7X_LANES)

    def mean_pass(r, carry):
        mu_ref[_ln_rows(r), :] = jnp.broadcast_to(_row_sum(load(r)) * inv_d, stat)
        return carry

    def var_pass(r, carry):
        d = load(r) - mu_ref[_ln_rows(r), 0:1]
        rs_ref[_ln_rows(r), :] = jnp.broadcast_to(lax.rsqrt(_row_sum(d * d) * inv_d + LN_EPS), stat)
        return carry

    lax.fori_loop(0, n, mean_pass, 0, unroll=LN_UNROLL)
    lax.fori_loop(0, n, var_pass, 0, unroll=LN_UNROLL)


def _ln_apply(zr, r, mu_ref, rs_ref, g_ref, b_ref):
    return (zr - mu_ref[_ln_rows(r), 0:1]) * rs_ref[_ln_rows(r), 0:1] * g_ref[...] + b_ref[...]


def _ln_body(z_ref, g_ref, b_ref, ohb_ref, mu_ref, rs_ref):
    tr, D = z_ref.shape
    n = tr // LN_ROWS

    def load(r):
        return z_ref[_ln_rows(r), :]

    _ln_stats(load, n, mu_ref, rs_ref, D)

    def norm_pass(r, carry):
        ohb_ref[_ln_rows(r), :] = _ln_apply(load(r), r, mu_ref, rs_ref, g_ref, b_ref).astype(BF16)
        return carry

    lax.fori_loop(0, n, norm_pass, 0, unroll=2)


def _layer_norm(z, gamma, beta, name, tr_target=256):
    M, D = z.shape
    tr = _pick_tile(M, tr_target, LN_ROWS * LN_UNROLL)
    blk = _nbytes((tr, D), F32) + _nbytes((tr, D), BF16)
    stat = pl.BlockSpec((tr, V7X_LANES), lambda i: (i, 0))
    return pl.pallas_call(
        _ln_body,
        grid=(M // tr,),
        in_specs=[pl.BlockSpec((tr, D), lambda i: (i, 0)),
                  pl.BlockSpec((1, D), lambda i: (0, 0)),
                  pl.BlockSpec((1, D), lambda i: (0, 0))],
        out_specs=[pl.BlockSpec((tr, D), lambda i: (i, 0)), stat, stat],
        out_shape=[jax.ShapeDtypeStruct((M, D), BF16),
                   jax.ShapeDtypeStruct((M, V7X_LANES), F32),
                   jax.ShapeDtypeStruct((M, V7X_LANES), F32)],
        compiler_params=_params(("parallel",), blk),
        name=name,
    )(z, gamma.reshape(1, D), beta.reshape(1, D))


def _ln_final_body(za_ref, zb_ref, g_ref, b_ref, o_ref, mu_ref, rs_ref, *, shift):
    tr, D = o_ref.shape[1], o_ref.shape[2]
    n = tr // LN_ROWS

    def load(r):
        return za_ref[0, _ln_rows(r), :]

    _ln_stats(load, n, mu_ref, rs_ref, D)

    def norm_pass(r, carry):
        o_ref[0, _ln_rows(r - shift), :] = _ln_apply(load(r), r, mu_ref, rs_ref, g_ref, b_ref)
        return carry

    lax.fori_loop(shift, n, norm_pass, 0)
    for c in range(shift):
        zr = zb_ref[0, c * LN_ROWS:(c + 1) * LN_ROWS, :]
        mu = _row_sum(zr) * (1.0 / D)
        d = zr - mu
        rs = lax.rsqrt(_row_sum(d * d) * (1.0 / D) + LN_EPS)
        o_ref[0, (n - shift + c) * LN_ROWS:(n - shift + c + 1) * LN_ROWS, :] = d * rs * g_ref[...] + b_ref[...]


def _layer_norm_final(z, gamma, beta, B, T_pad, n_meta, S, name, tr_target=256):
    D = z.shape[1]
    assert n_meta % LN_ROWS == 0
    tr = _pick_tile(S, tr_target, LN_ROWS * LN_UNROLL)
    assert tr % n_meta == 0
    per = tr // n_meta
    blk = 2 * _nbytes((tr, D), F32) + _nbytes((n_meta, D), F32)
    z3 = z.reshape(B, T_pad, D)
    return pl.pallas_call(
        functools.partial(_ln_final_body, shift=n_meta // LN_ROWS),
        grid=(B, S // tr),
        in_specs=[pl.BlockSpec((1, tr, D), lambda b, t: (b, t, 0)),
                  pl.BlockSpec((1, n_meta, D), lambda b, t: (b, (t + 1) * per, 0)),
                  pl.BlockSpec((1, D), lambda b, t: (0, 0)),
                  pl.BlockSpec((1, D), lambda b, t: (0, 0))],
        out_specs=pl.BlockSpec((1, tr, D), lambda b, t: (b, t, 0)),
        out_shape=jax.ShapeDtypeStruct((B, S, D), F32),
        scratch_shapes=[pltpu.VMEM((tr, V7X_LANES), F32), pltpu.VMEM((tr, V7X_LANES), F32)],
        compiler_params=_params(("parallel", "parallel"), blk),
        name=name,
    )(z3, z3, gamma.reshape(1, D), beta.reshape(1, D))


SEG = V7X_SUBLANES
GROUP = SEG * SEG


def _sub_shift(v, d, fill, sub):
    return jnp.where(sub >= d, pltpu.roll(v, d, 0), fill)


def _rglru_body(*refs, conv_w, hpb, blk, nl):
    u_refs = refs[:nl]
    g_ref, cw_ref, cb_ref, wg_ref, bg_ref, lam_ref, y_ref = refs[nl:nl + 7]
    ucb, sa, sx, hs, tail, hcar = refs[nl + 7:]
    t = pl.program_id(2)
    tt = g_ref.shape[0]
    ngroup = tt // GROUP
    sub = lax.broadcasted_iota(jnp.int32, (SEG, V7X_LANES), 0)
    tile = (SEG, V7X_LANES)

    @pl.when(t == 0)
    def _():
        tail[...] = jnp.zeros_like(tail)
        hcar[...] = jnp.zeros_like(hcar)

    for c in range(nl):
        lanes = slice(c * V7X_LANES, (c + 1) * V7X_LANES)
        wk = [jnp.broadcast_to(cw_ref[k:k + 1, lanes], tile) for k in range(conv_w)]
        cb = jnp.broadcast_to(cb_ref[:, lanes], tile)
        prev = [tail[c, m] for m in range(conv_w - 1)]
        for gi in range(ngroup):
            cur = [u_refs[c][pl.ds(gi * GROUP + j, SEG, stride=SEG), :] for j in range(SEG)]
            ext = [jnp.where(sub == 0, pltpu.roll(prev[m], 1, 0), pltpu.roll(cur[SEG - (conv_w - 1) + m], 1, 0))
                   for m in range(conv_w - 1)]
            seq = ext + cur
            for j in range(SEG):
                acc = cb
                for k in range(conv_w):
                    acc = acc + wk[k] * seq[j + k]
                ucb[gi * GROUP + j * SEG:gi * GROUP + (j + 1) * SEG, lanes] = acc
            prev = cur[SEG - (conv_w - 1):]
        for m in range(conv_w - 1):
            tail[c, m] = prev[m]

    lam = lam_ref[...]
    logsig = -(jnp.maximum(-lam, 0.0) + jnp.log1p(jnp.exp(-jnp.abs(lam))))
    for hh in range(hpb):
        cols = slice(hh * blk, (hh + 1) * blk)
        ucs = ucb[:, cols]
        gate = jnp.dot(ucs.astype(BF16), wg_ref[hh], preferred_element_type=F32) + bg_ref[hh]
        r = _sigmoid(gate[:, :blk])
        i = _sigmoid(gate[:, blk:])
        a = jnp.exp(LRU_C * r * logsig[:, cols])
        sa[:, cols] = a
        sx[:, cols] = (ucs * i) * jnp.sqrt(1.0 - a * a)

    for c in range(nl):
        lanes = slice(c * V7X_LANES, (c + 1) * V7X_LANES)
        h_in = hcar[c]
        for gi in range(ngroup):
            hloc, ploc = [], []
            for j in range(SEG):
                rows = slice(gi * GROUP + j * SEG, gi * GROUP + (j + 1) * SEG)
                a = sa[rows, lanes]
                x = sx[rows, lanes]
                hloc.append(x if j == 0 else a * hloc[-1] + x)
                ploc.append(a if j == 0 else a * ploc[-1])
            at, xt = ploc[-1], hloc[-1]
            d = 1
            while d < SEG:
                xt = at * _sub_shift(xt, d, 0.0, sub) + xt
                at = at * _sub_shift(at, d, 1.0, sub)
                d *= 2
            carry = jnp.where(sub == 0, h_in, pltpu.roll(xt, 1, 0) + pltpu.roll(at, 1, 0) * h_in)
            for j in range(SEG):
                h = hloc[j] + ploc[j] * carry
                hs[c, pl.ds(gi * GROUP + j, SEG, stride=SEG), :] = h
            h_in = jnp.broadcast_to(h[SEG - 1:SEG, :], tile)
        hcar[c] = h_in

    for c in range(nl):
        lanes = slice(c * V7X_LANES, (c + 1) * V7X_LANES)
        y_ref[:, lanes] = (hs[c] * _silu(g_ref[:, lanes])).astype(BF16)


def _rglru_mix(ug, conv_w, conv_b, w_a, b_a, w_x, b_x, lam, B, T_pad):
    BR = conv_b.shape[0]
    heads, blk, _ = w_a.shape
    K = conv_w.shape[0]
    hpb = 2 if heads % 2 == 0 else 1
    cw = hpb * blk
    nl = cw // V7X_LANES
    ncb = BR // cw
    tt = _pick_tile(T_pad, 384, GROUP)
    nT = T_pad // tt
    wg = jnp.concatenate([w_a, w_x], axis=-1).astype(BF16)
    bg = jnp.concatenate([b_a.reshape(heads, 1, blk), b_x.reshape(heads, 1, blk)], axis=-1)
    body = functools.partial(_rglru_body, conv_w=K, hpb=hpb, blk=blk, nl=nl)
    blkbytes = 7 * _nbytes((tt, cw), F32) + _nbytes((hpb, blk, 2 * blk), BF16)

    def u_spec(l):
        return pl.BlockSpec((tt, V7X_LANES), lambda b, c, t: (b * nT + t, c * nl + l))

    return pl.pallas_call(
        body,
        grid=(B, ncb, nT),
        in_specs=[u_spec(l) for l in range(nl)] + [
            pl.BlockSpec((tt, cw), lambda b, c, t: (b * nT + t, c + ncb)),
            pl.BlockSpec((K, cw), lambda b, c, t: (0, c)),
            pl.BlockSpec((1, cw), lambda b, c, t: (0, c)),
            pl.BlockSpec((hpb, blk, 2 * blk), lambda b, c, t: (c, 0, 0)),
            pl.BlockSpec((hpb, 1, 2 * blk), lambda b, c, t: (c, 0, 0)),
            pl.BlockSpec((1, cw), lambda b, c, t: (0, c))],
        out_specs=pl.BlockSpec((tt, cw), lambda b, c, t: (b * nT + t, c)),
        out_shape=jax.ShapeDtypeStruct((B * T_pad, BR), BF16),
        scratch_shapes=[pltpu.VMEM((tt, cw), F32),
                        pltpu.VMEM((tt, cw), F32),
                        pltpu.VMEM((tt, cw), F32),
                        pltpu.VMEM((nl, tt, V7X_LANES), F32),
                        pltpu.VMEM((nl, K - 1, SEG, V7X_LANES), F32),
                        pltpu.VMEM((nl, SEG, V7X_LANES), F32)],
        compiler_params=_params(("parallel", "parallel", "arbitrary"), blkbytes),
        name="rglru_mix",
    )(*([ug] * nl), ug, conv_w.reshape(K, BR), conv_b.reshape(1, BR), wg, bg, lam.reshape(1, BR))


POOL_HIST = 24


def _pool_body(u_ref, g_ref, w_ref, sc_ref, y_ref, ubuf, sbuf, pbuf, *, windows):
    grp = pl.program_id(0)
    t = pl.program_id(2)
    tt = u_ref.shape[1]
    gw = u_ref.shape[2]
    n = POOL_HIST + tt

    @pl.when(t == 0)
    def _():
        ubuf[0:POOL_HIST, :] = jnp.zeros((POOL_HIST, gw), F32)

    sbuf[0:V7X_SUBLANES, :] = jnp.zeros((V7X_SUBLANES, gw), F32)
    ubuf[POOL_HIST:n, :] = u_ref[0]
    pos = lax.broadcasted_iota(jnp.int32, (tt, 1), 0) + t * tt + 1

    for gi, w in enumerate(windows):
        @pl.when(grp == gi)
        def _(w=w):
            lo = V7X_SUBLANES
            s = ubuf[lo:n, :] + ubuf[lo - 1:n - 1, :]
            d = 2
            while d < w:
                sbuf[lo:n, :] = s
                s = sbuf[lo:n, :] + sbuf[lo - d:n - d, :]
                d *= 2
            wsum = s[POOL_HIST - lo:, :]
            cnt = jnp.minimum(pos, w).astype(F32)
            pbuf[...] = (wsum / cnt - u_ref[0]).astype(BF16)

    ubuf[0:POOL_HIST, :] = ubuf[tt:n, :]
    mixed = jnp.dot(pbuf[...], w_ref[0], preferred_element_type=F32) * sc_ref[...]
    y_ref[0] = (mixed * _silu(g_ref[0])).astype(BF16)


def _pool_mix(ug, w_grp, scale, B, T_pad):
    ngrp, gw, _ = w_grp.shape
    BR = ngrp * gw
    tt = _pick_tile(T_pad, 384, V7X_BF16_ROWS)
    ug3 = ug.reshape(B, T_pad, 2 * BR)
    body = functools.partial(_pool_body, windows=POOL_WINDOWS[:ngrp])
    blkbytes = 5 * _nbytes((tt, gw), F32) + _nbytes((gw, gw), BF16)
    return pl.pallas_call(
        body,
        grid=(ngrp, B, T_pad // tt),
        in_specs=[pl.BlockSpec((1, tt, gw), lambda g, b, t: (b, t, g)),
                  pl.BlockSpec((1, tt, gw), lambda g, b, t: (b, t, g + ngrp)),
                  pl.BlockSpec((1, gw, gw), lambda g, b, t: (g, 0, 0)),
                  pl.BlockSpec((1, gw), lambda g, b, t: (0, g))],
        out_specs=pl.BlockSpec((1, tt, gw), lambda g, b, t: (b, t, g)),
        out_shape=jax.ShapeDtypeStruct((B, T_pad, BR), BF16),
        scratch_shapes=[pltpu.VMEM((POOL_HIST + tt, gw), F32),
                        pltpu.VMEM((POOL_HIST + tt, gw), F32),
                        pltpu.VMEM((tt, gw), BF16)],
        compiler_params=_params(("parallel", "parallel", "arbitrary"), blkbytes),
        name="pool_mix",
    )(ug3, ug3, w_grp.astype(BF16), scale.reshape(1, BR))


def _rope_pairs(x, c_ref, s_ref):
    half = MLA_ROPE // 2
    lane = lax.broadcasted_iota(jnp.int32, x.shape, 1)
    partner = jnp.where(lane % MLA_ROPE < half,
                        pltpu.roll(x, V7X_LANES - half, 1), pltpu.roll(x, half, 1))
    return x * c_ref[...] + partner * s_ref[...]


def _rms_scale(x, g):
    return x * lax.rsqrt(jnp.mean(jnp.square(x), axis=-1, keepdims=True) + RMS_EPS) * g


def _qproj_body(c_ref, n_ref, w_ref, cos_ref, sin_ref, q_ref, xn_ref, *, ppb):
    @pl.when(pl.program_id(1) == 0)
    def _():
        xn_ref[...] = _rms_scale(c_ref[...], n_ref[...]).astype(BF16)

    res = jnp.dot(xn_ref[...], w_ref[...], preferred_element_type=F32)
    nn = 2 * MLA_NOPE
    pw = nn + 2 * MLA_ROPE
    for p in range(ppb):
        q_ref[:, p * pw:p * pw + nn] = res[:, p * pw:p * pw + nn].astype(BF16)
        q_ref[:, p * pw + nn:(p + 1) * pw] = _rope_pairs(
            res[:, p * pw + nn:(p + 1) * pw], cos_ref, sin_ref).astype(BF16)


def _kvproj_body(c_ref, kr_ref, n_ref, w_ref, cos_ref, sin_ref, kv_ref, krd_ref, xn_ref):
    @pl.when(pl.program_id(1) == 0)
    def _():
        xn_ref[...] = _rms_scale(c_ref[...], n_ref[...]).astype(BF16)
        rot = _rope_pairs(kr_ref[...], cos_ref, sin_ref)
        lane = lax.broadcasted_iota(jnp.int32, rot.shape, 1)
        krd_ref[...] = jnp.where(lane < MLA_ROPE, rot, pltpu.roll(rot, MLA_ROPE, 1)).astype(BF16)

    kv_ref[...] = jnp.dot(xn_ref[...], w_ref[...], preferred_element_type=F32).astype(BF16)


def _attn_body(q_ref, kn_ref, v_ref, kr_ref, g_ref, y_ref, m_ref, acc_ref, *, scale2, pps):
    qi = pl.program_id(2)
    tq = q_ref.shape[1]
    wide = 2 * tq
    nn = 2 * MLA_NOPE
    pw = nn + 2 * MLA_ROPE
    heads = [(pp, hh) for pp in range(pps) for hh in range(2)]
    lane = lax.broadcasted_iota(jnp.int32, (tq, V7X_LANES), 1)
    qs = []
    for pp, hh in heads:
        qr = q_ref[0, :, pp * pw + nn:(pp + 1) * pw]
        mine = (lane < MLA_ROPE) if hh == 0 else (lane >= MLA_ROPE)
        nope = q_ref[0, :, pp * pw + hh * MLA_NOPE:pp * pw + (hh + 1) * MLA_NOPE]
        qs.append(jnp.concatenate([nope, jnp.where(mine, qr, jnp.zeros_like(qr))], axis=1))

    m_ref[...] = jnp.full_like(m_ref, -jnp.inf)
    acc_ref[...] = jnp.zeros_like(acc_ref)

    def block(k0, tk, mask_shift):
        krope = kr_ref[0, pl.ds(k0, tk), :]
        ones = jnp.ones((tk, V7X_LANES), BF16)
        nc = tk // V7X_LANES
        for h, (pp, hh) in enumerate(heads):
            hc = slice((2 * pp + hh) * MLA_NOPE, (2 * pp + hh + 1) * MLA_NOPE)
            kf = jnp.concatenate([kn_ref[0, pl.ds(k0, tk), hc], krope], axis=1)
            s = lax.dot_general(qs[h], kf, (((1,), (1,)), ((), ())),
                                preferred_element_type=F32) * scale2
            if mask_shift is not None:
                rq = lax.broadcasted_iota(jnp.int32, (tq, tk), 0)
                ck = lax.broadcasted_iota(jnp.int32, (tq, tk), 1)
                s = jnp.where(ck <= rq + mask_shift, s, -1e30)
            cols = [s[:, c * V7X_LANES:(c + 1) * V7X_LANES] for c in range(nc)]
            mloc = cols[0]
            for c in range(1, nc):
                mloc = jnp.maximum(mloc, cols[c])
            m_old = m_ref[h]
            m_new = jnp.maximum(m_old, jnp.max(mloc, axis=1, keepdims=True))
            p = jnp.concatenate([jnp.exp2(cols[c] - m_new) for c in range(nc)], axis=1).astype(BF16)
            corr = jnp.exp2(m_old - m_new)
            vf = jnp.concatenate([v_ref[0, pl.ds(k0, tk), hc], ones], axis=1)
            pv = jnp.dot(p, vf, preferred_element_type=F32)
            acc_ref[h] = jnp.concatenate([corr, corr], axis=1) * acc_ref[h] + pv
            m_ref[h] = m_new

    nblk = qi // 2

    def loop_body(j, carry):
        block(pl.multiple_of(2 * j * wide, wide), wide, None)
        block(pl.multiple_of((2 * j + 1) * wide, wide), wide, None)
        return carry

    lax.fori_loop(0, nblk // 2, loop_body, 0)

    @pl.when(nblk % 2 == 1)
    def _():
        block(pl.multiple_of((nblk - 1) * wide, wide), wide, None)

    @pl.when(qi % 2 == 1)
    def _():
        block(pl.multiple_of((qi - 1) * tq, tq), wide, tq)

    @pl.when(qi % 2 == 0)
    def _():
        block(pl.multiple_of(qi * tq, tq), tq, 0)

    gate = _silu(g_ref[0])
    for h, (pp, hh) in enumerate(heads):
        hc = slice((2 * pp + hh) * MLA_NOPE, (2 * pp + hh + 1) * MLA_NOPE)
        acc = acc_ref[h]
        o = acc[:, :MLA_NOPE] / acc[:, MLA_NOPE:]
        y_ref[0, :, hc] = (o * gate[:, hc]).astype(BF16)


def _mla_mix(hb, w_in_p, q_norm, w_uq, kv_norm, w_ukv, B, T_pad):
    q_lora = q_norm.shape[0]
    kv_lora = kv_norm.shape[0]
    heads = w_uq.shape[1] // (MLA_NOPE + MLA_ROPE)
    vdim = w_ukv.shape[1] // heads - MLA_NOPE
    assert vdim == MLA_NOPE and heads % 2 == 0
    BR = heads * vdim
    npair = heads // 2
    M = hb.shape[0]
    D = hb.shape[1]

    c_all = _matmul(hb, w_in_p, F32, "mla_in", tn_target=1152)
    n_all = w_in_p.shape[1]

    tm = _pick_tile(T_pad, 1056, V7X_BF16_ROWS)
    nt = T_pad // tm

    half = MLA_ROPE // 2
    inv = ROPE_BASE ** (-jnp.arange(0, MLA_ROPE, 2, dtype=F32) / MLA_ROPE)
    ang = jnp.arange(T_pad, dtype=F32)[:, None] * inv[None, :]
    cos_t = jnp.tile(jnp.cos(ang), (1, 4))
    sin_t = jnp.tile(jnp.concatenate([-jnp.sin(ang), jnp.sin(ang)], axis=1), (1, 2))
    del half

    hd = MLA_NOPE + MLA_ROPE
    wq = w_uq.reshape(q_lora, npair, 2, hd)
    wq = jnp.concatenate([wq[..., 0, :MLA_NOPE], wq[..., 1, :MLA_NOPE],
                          wq[..., 0, MLA_NOPE:], wq[..., 1, MLA_NOPE:]], axis=-1)
    pw = 2 * hd
    wq = wq.reshape(q_lora, npair * pw).astype(BF16)
    cq_blk = BR // q_lora
    ppb = _pick_tile(npair, 4, 1)
    qw = ppb * pw
    q = pl.pallas_call(
        functools.partial(_qproj_body, ppb=ppb),
        grid=(M // tm, npair // ppb),
        in_specs=[pl.BlockSpec((tm, q_lora), lambda i, j: (i, cq_blk)),
                  pl.BlockSpec((1, q_lora), lambda i, j: (0, 0)),
                  pl.BlockSpec((q_lora, qw), lambda i, j: (0, j)),
                  pl.BlockSpec((tm, V7X_LANES), lambda i, j: (i % nt, 0)),
                  pl.BlockSpec((tm, V7X_LANES), lambda i, j: (i % nt, 0))],
        out_specs=pl.BlockSpec((tm, qw), lambda i, j: (i, j)),
        out_shape=jax.ShapeDtypeStruct((M, npair * pw), BF16),
        scratch_shapes=[pltpu.VMEM((tm, q_lora), BF16)],
        compiler_params=_params(("parallel", "arbitrary"),
                                _nbytes((tm, q_lora), F32) + _nbytes((q_lora, qw), BF16)
                                + 2 * _nbytes((tm, qw), F32)),
        name="mla_qproj",
    )(c_all, q_norm.reshape(1, q_lora), wq, cos_t, sin_t)

    wkv = w_ukv.reshape(kv_lora, heads, MLA_NOPE + vdim)
    wkv = jnp.concatenate([wkv[..., :MLA_NOPE].reshape(kv_lora, BR),
                           wkv[..., MLA_NOPE:].reshape(kv_lora, BR)], axis=1).astype(BF16)
    tn = _pick_tile(2 * BR, 1024, V7X_LANES)
    ckv_blk = (BR + q_lora) // kv_lora
    kr_blk = (BR + q_lora + kv_lora) // V7X_LANES
    kv, krd = pl.pallas_call(
        _kvproj_body,
        grid=(M // tm, 2 * BR // tn),
        in_specs=[pl.BlockSpec((tm, kv_lora), lambda i, j: (i, ckv_blk)),
                  pl.BlockSpec((tm, V7X_LANES), lambda i, j: (i, kr_blk)),
                  pl.BlockSpec((1, kv_lora), lambda i, j: (0, 0)),
                  pl.BlockSpec((kv_lora, tn), lambda i, j: (0, j)),
                  pl.BlockSpec((tm, V7X_LANES), lambda i, j: (i % nt, 0)),
                  pl.BlockSpec((tm, V7X_LANES), lambda i, j: (i % nt, 0))],
        out_specs=[pl.BlockSpec((tm, tn), lambda i, j: (i, j)),
                   pl.BlockSpec((tm, V7X_LANES), lambda i, j: (i, 0))],
        out_shape=[jax.ShapeDtypeStruct((M, 2 * BR), BF16),
                   jax.ShapeDtypeStruct((M, V7X_LANES), BF16)],
        scratch_shapes=[pltpu.VMEM((tm, kv_lora), BF16)],
        compiler_params=_params(("parallel", "arbitrary"),
                                _nbytes((tm, kv_lora), F32) + _nbytes((kv_lora, tn), BF16)
                                + 2 * _nbytes((tm, tn), F32)),
        name="mla_kvproj",
    )(c_all, c_all, kv_norm.reshape(1, kv_lora), wkv, cos_t, sin_t)

    tq = _pick_tile(T_pad, 384, V7X_BF16_ROWS)
    pps = _pick_tile(npair, 4, 1)
    ngrp = npair // pps
    nh = 2 * pps
    pv = nh * vdim
    qw = pps * pw
    q3 = q.reshape(B, T_pad, npair * pw)
    kv3 = kv.reshape(B, T_pad, 2 * BR)
    krd3 = krd.reshape(B, T_pad, V7X_LANES)
    c3 = c_all.reshape(B, T_pad, n_all)
    body = functools.partial(_attn_body, scale2=float(hd) ** -0.5 * LOG2_E, pps=pps)
    blkbytes = (_nbytes((tq, qw), BF16) + 2 * _nbytes((T_pad, pv), BF16) + _nbytes((T_pad, V7X_LANES), BF16)
                + _nbytes((tq, pv), F32) + _nbytes((tq, pv), BF16) + 2 * nh * _nbytes((tq, 2 * tq), F32))
    return pl.pallas_call(
        body,
        grid=(B, ngrp, T_pad // tq),
        in_specs=[pl.BlockSpec((1, tq, qw), lambda b, p, i: (b, i, p)),
                  pl.BlockSpec((1, T_pad, pv), lambda b, p, i: (b, 0, p)),
                  pl.BlockSpec((1, T_pad, pv), lambda b, p, i: (b, 0, p + ngrp)),
                  pl.BlockSpec((1, T_pad, V7X_LANES), lambda b, p, i: (b, 0, 0)),
                  pl.BlockSpec((1, tq, pv), lambda b, p, i: (b, i, p))],
        out_specs=pl.BlockSpec((1, tq, pv), lambda b, p, i: (b, i, p)),
        out_shape=jax.ShapeDtypeStruct((B, T_pad, BR), BF16),
        scratch_shapes=[pltpu.VMEM((nh, tq, V7X_LANES), F32),
                        pltpu.VMEM((nh, tq, 2 * vdim), F32)],
        compiler_params=_params(("parallel", "parallel", "arbitrary"), blkbytes),
        name="mla_attn",
    )(q3, kv3, kv3, krd3, c3).reshape(M, BR)


def _mm_rope_body(a_ref, w_ref, cos_ref, sin_ref, o_ref, *, dk, scale):
    res = jnp.dot(a_ref[...], w_ref[...], preferred_element_type=F32)
    cos = cos_ref[...]
    sin = sin_ref[...]
    hd = dk // 2
    for h in range(o_ref.shape[1] // dk):
        x1 = res[:, h * dk:h * dk + hd]
        x2 = res[:, h * dk + hd:(h + 1) * dk]
        o_ref[:, h * dk:h * dk + hd] = ((x1 * cos - x2 * sin) * scale).astype(o_ref.dtype)
        o_ref[:, h * dk + hd:(h + 1) * dk] = ((x2 * cos + x1 * sin) * scale).astype(o_ref.dtype)


def _matmul_rope(a, w, cols, cos_t, sin_t, dk, scale, out_dtype, name, tm_target=1056, tn_target=1024):
    M, K = a.shape
    start, N = cols
    T_pad = cos_t.shape[0]
    tm = _pick_tile(T_pad, tm_target, V7X_BF16_ROWS)
    tn = _pick_tile(N, tn_target, dk)
    assert start % tn == 0
    j0 = start // tn
    nt = T_pad // tm
    blk = (_nbytes((tm, K), a.dtype) + _nbytes((K, tn), w.dtype) + 2 * _nbytes((tm, tn), F32)
           + 2 * _nbytes((tm, dk // 2), F32))
    return pl.pallas_call(
        functools.partial(_mm_rope_body, dk=dk, scale=scale),
        grid=(M // tm, N // tn),
        in_specs=[pl.BlockSpec((tm, K), lambda i, j: (i, 0)),
                  pl.BlockSpec((K, tn), lambda i, j: (0, j + j0)),
                  pl.BlockSpec((tm, dk // 2), lambda i, j: (i % nt, 0)),
                  pl.BlockSpec((tm, dk // 2), lambda i, j: (i % nt, 0))],
        out_specs=pl.BlockSpec((tm, tn), lambda i, j: (i, j)),
        out_shape=jax.ShapeDtypeStruct((M, N), out_dtype),
        compiler_params=_params(("parallel", "arbitrary"), blk),
        name=name,
    )(a, w, cos_t, sin_t)


def _ret_body(lg_ref, q_ref, k_ref, v_ref, g_ref, y_ref, st_ref, dec_ref, *, hps, dk):
    c = pl.program_id(2)
    C = q_ref.shape[1]

    @pl.when(c == 0)
    def _():
        st_ref[...] = jnp.zeros_like(st_ref)
        ri = lax.broadcasted_iota(jnp.int32, (C, C), 0)
        ci = lax.broadcasted_iota(jnp.int32, (C, C), 1)
        diff = (ri - ci).astype(F32)
        for h in range(hps):
            dec_ref[h] = jnp.where(diff >= 0, jnp.exp(jnp.maximum(diff, 0.0) * lg_ref[h, :, 0:1]), 0.0)

    jcol = lax.broadcasted_iota(jnp.int32, (C, 1), 0).astype(F32)
    for h in range(hps):
        cols = slice(h * dk, (h + 1) * dk)
        lg = lg_ref[h, :, 0:1]
        qb = q_ref[0, :, cols]
        kf = k_ref[0, :, cols]
        vb = v_ref[0, :, cols]
        s = lax.dot_general(qb, kf.astype(BF16), (((1,), (1,)), ((), ())), preferred_element_type=F32)
        intra = jnp.dot((s * dec_ref[h]).astype(BF16), vb, preferred_element_type=F32)
        state = st_ref[h]
        inter = jnp.dot(qb, state.astype(BF16), preferred_element_type=F32) * jnp.exp((jcol + 1.0) * lg)
        kw = (kf * jnp.exp((C - 1.0 - jcol) * lg)).astype(BF16)
        st_ref[h] = jnp.exp(C * lg) * state + lax.dot_general(
            kw, vb, (((0,), (0,)), ((), ())), preferred_element_type=F32)
        o = intra + inter
        mu = jnp.mean(o, axis=-1, keepdims=True)
        var = jnp.mean(jnp.square(o - mu), axis=-1, keepdims=True)
        o = (o - mu) * lax.rsqrt(var + LN_EPS)
        y_ref[0, :, cols] = (o * _silu(g_ref[0, :, cols])).astype(BF16)


def _ret_mix(hb, wb, heads, B, T_pad):
    BR = wb.shape[1] // 4
    dk = BR // heads
    C = _pick_tile(T_pad, 384, V7X_BF16_ROWS)
    hps = 2 if heads % 2 == 0 else 1
    inv = ROPE_BASE ** (-jnp.arange(0, dk, 2, dtype=F32) / dk)
    ang = jnp.arange(T_pad, dtype=F32)[:, None] * inv[None, :]
    cos_t = jnp.cos(ang)
    sin_t = jnp.sin(ang)
    q =_matmul_rope(hb, wb, (0, BR), cos_t, sin_t, dk, 1.0, BF16, "l3_q")
    k = _matmul_rope(hb, wb, (BR, BR), cos_t, sin_t, dk, float(dk) ** -0.5, F32, "l3_k")
    v = _matmul(hb, wb, BF16, "l3_v", cols=(2 * BR, BR))
    g = _matmul(hb, wb, F32, "l3_g", cols=(3 * BR, BR))

    log_g = jnp.log(1.0 - 2.0 ** (-5.0 - jnp.arange(heads, dtype=F32)))
    lg = jnp.broadcast_to(log_g[:, None, None], (heads, 1, V7X_LANES))
    cw = hps * dk
    spec = pl.BlockSpec((1, C, cw), lambda b, h, c: (b, c, h))
    blkbytes = (2 * _nbytes((C, cw), F32) + 3 * _nbytes((C, cw), BF16)
                + hps * (4 * _nbytes((C, C), F32) + 2 * _nbytes((dk, dk), F32)))
    return pl.pallas_call(
        functools.partial(_ret_body, hps=hps, dk=dk),
        grid=(B, heads // hps, T_pad // C),
        in_specs=[pl.BlockSpec((hps, 1, V7X_LANES), lambda b, h, c: (h, 0, 0)), spec, spec, spec, spec],
        out_specs=spec,
        out_shape=jax.ShapeDtypeStruct((B, T_pad, BR), BF16),
        scratch_shapes=[pltpu.VMEM((hps, dk, dk), F32), pltpu.VMEM((hps, C, C), F32)],
        compiler_params=_params(("parallel", "parallel", "arbitrary"), blkbytes),
        name="ret_mix",
    )(lg, q.reshape(B, T_pad, BR), k.reshape(B, T_pad, BR), v.reshape(B, T_pad, BR),
      g.reshape(B, T_pad, BR)).reshape(B * T_pad, BR)


RET_HEADS = 16


def kernel(x, meta_tokens, l0_w_in, l0_conv_w, l0_conv_b, l0_w_a, l0_b_a, l0_w_x, l0_b_x, l0_lam, l0_w_out, l0_ln_g, l0_ln_b, l1_w_in, l1_w_grp, l1_scale, l1_w_out, l1_ln_g, l1_ln_b, l2_w_in, l2_q_norm, l2_w_uq, l2_kv_norm, l2_w_ukv, l2_w_out, l2_ln_g, l2_ln_b, l3_w_in, l3_w_out, l3_ln_g, l3_ln_b):
    B, S, D = x.shape
    n_meta = meta_tokens.shape[0]
    T = n_meta + S
    T_pad = -(-T // TIME_ALIGN) * TIME_ALIGN
    M = B * T_pad

    meta = jnp.broadcast_to(meta_tokens.astype(x.dtype)[None], (B, n_meta, D))
    h = jnp.concatenate([meta, x, jnp.zeros((B, T_pad - T, D), x.dtype)], axis=1).reshape(M, D)
    hb = h.astype(BF16)

    ug, (w0_out, w1_in) = _matmul(hb, l0_w_in.astype(BF16), F32, "l0_in", casts=(l0_w_out, l1_w_in))
    y = _rglru_mix(ug, l0_conv_w, l0_conv_b, l0_w_a, l0_b_a, l0_w_x, l0_b_x, l0_lam, B, T_pad)
    z, (w1_out,) = _out_proj(y.reshape(M, -1), w0_out, (h,), "l0_out_mm", casts=(l1_w_out,))
    hb, mu, rs = _layer_norm(z, l0_ln_g, l0_ln_b, "l0_out_ln")
    res = (z, mu, rs, l0_ln_g, l0_ln_b)

    ug, (w2_in, w2_out) = _matmul(hb, w1_in, F32, "l1_in", casts=(l2_w_in, l2_w_out))
    w2_in = jnp.pad(w2_in, ((0, 0), (0, -w2_in.shape[1] % V7X_LANES)))
    y = _pool_mix(ug, l1_w_grp, l1_scale, B, T_pad)
    z, (w3_in,) = _out_proj(y.reshape(M, -1), w1_out, res, "l1_out_mm", casts=(l3_w_in,))
    hb, mu, rs = _layer_norm(z, l1_ln_g, l1_ln_b, "l1_out_ln")
    res = (z, mu, rs, l1_ln_g, l1_ln_b)

    y = _mla_mix(hb, w2_in, l2_q_norm, l2_w_uq, l2_kv_norm, l2_w_ukv, B, T_pad)
    z, (w3_out,) = _out_proj(y, w2_out, res, "l2_out_mm", casts=(l3_w_out,))
    hb, mu, rs = _layer_norm(z, l2_ln_g, l2_ln_b, "l2_out_ln")
    res = (z, mu, rs, l2_ln_g, l2_ln_b)

    y = _ret_mix(hb, w3_in, RET_HEADS, B, T_pad)
    z, _ = _out_proj(y, w3_out, res, "l3_out_mm")
    return _layer_norm_final(z, l3_ln_g, l3_ln_b, B, T_pad, n_meta, S, "l3_out_ln")
```

```python
import functools

import jax
import jax.numpy as jnp
from jax import lax
from jax.experimental import pallas as pl
from jax.experimental.pallas import tpu as pltpu

F32 = jnp.float32
BF16 = jnp.bfloat16

DEPTH = 4
ALPHA = (2.0 * DEPTH) ** 0.25
LN_EPS = 1e-5
RMS_EPS = 1e-6
ROPE_BASE = 10000.0
LRU_C = 8.0
POOL_WINDOWS = (2, 4, 8, 16)
MLA_NOPE = 128
MLA_ROPE = 64
LOG2_E = 1.4426950408889634

V7X_LANES = 128
V7X_SUBLANES = 8
V7X_BF16_ROWS = 16
V7X_SCOPED_VMEM_MAX = 60000 * 1024
V7X_VMEM_INTERNAL = 12 * 1024 * 1024

TIME_ALIGN = 128


def _pick_tile(n, target, mult):
    best = None
    for d in range(mult, min(n, target) + 1, mult):
        if n % d == 0:
            best = d
    return n if best is None else best


def _params(semantics, block_bytes):
    limit = min(2 * block_bytes + V7X_VMEM_INTERNAL, V7X_SCOPED_VMEM_MAX)
    return pltpu.CompilerParams(dimension_semantics=semantics, vmem_limit_bytes=int(limit))


def _nbytes(shape, dtype):
    n = 1
    for s in shape:
        n *= s
    return n * jnp.dtype(dtype).itemsize


def _sigmoid(x):
    return 1.0 / (1.0 + jnp.exp(-x))


def _silu(x):
    return x * _sigmoid(x)


def _cast_specs(casts, grid):
    ni, nj = grid
    nsteps = ni * nj
    in_specs, out_specs, shapes, nbytes = [], [], [], 0
    for wt in casts:
        R, C = wt.shape
        rows = R // nsteps
        assert R % nsteps == 0 and rows % V7X_BF16_ROWS == 0
        spec = pl.BlockSpec((rows, C), lambda i, j: (i * nj + j, 0))
        in_specs.append(spec)
        out_specs.append(spec)
        shapes.append(jax.ShapeDtypeStruct((R, C), BF16))
        nbytes += _nbytes((rows, C), F32) + _nbytes((rows, C), BF16)
    return in_specs, out_specs, shapes, nbytes


def _cast_blocks(src_refs, dst_refs):
    for src, dst in zip(src_refs, dst_refs):
        dst[...] = src[...].astype(BF16)


def _mm_body(a_ref, w_ref, *rest):
    o_ref = rest[len(rest) // 2]
    o_ref[...] = jnp.dot(a_ref[...], w_ref[...], preferred_element_type=F32).astype(o_ref.dtype)
    _cast_blocks(rest[:len(rest) // 2], rest[len(rest) // 2 + 1:])


def _matmul(a, w, out_dtype, name, cols=None, casts=(), tm_target=1056, tn_target=1024):
    M, K = a.shape
    start, N = (0, w.shape[1]) if cols is None else cols
    tm = _pick_tile(M, tm_target, V7X_BF16_ROWS)
    tn = _pick_tile(N, tn_target, V7X_LANES)
    assert start % tn == 0
    j0 = start // tn
    grid = (M // tm, N // tn)
    c_in, c_out, c_shape, c_bytes = _cast_specs(casts, grid)
    blk = _nbytes((tm, K), a.dtype) + _nbytes((K, tn), w.dtype) + _nbytes((tm, tn), out_dtype) + c_bytes
    outs = pl.pallas_call(
        _mm_body,
        grid=grid,
        in_specs=[pl.BlockSpec((tm, K), lambda i, j: (i, 0)),
                  pl.BlockSpec((K, tn), lambda i, j: (0, j + j0))] + c_in,
        out_specs=[pl.BlockSpec((tm, tn), lambda i, j: (i, j))] + c_out,
        out_shape=[jax.ShapeDtypeStruct((M, N), out_dtype)] + c_shape,
        compiler_params=_params(("parallel", "arbitrary"), blk),
        name=name,
    )(a, w, *casts)
    return (outs[0], tuple(outs[1:])) if casts else outs[0]


def _mm_res_body(y_ref, w_ref, h_ref, *rest):
    o_ref = rest[len(rest) // 2]
    o_ref[...] = ALPHA * h_ref[...] + jnp.dot(y_ref[...], w_ref[...], preferred_element_type=F32)
    _cast_blocks(rest[:len(rest) // 2], rest[len(rest) // 2 + 1:])


def _mm_lnres_body(y_ref, w_ref, z_ref, mu_ref, rs_ref, g_ref, b_ref, *rest):
    o_ref = rest[len(rest) // 2]
    h = (z_ref[...] - mu_ref[:, 0:1]) * rs_ref[:, 0:1] * g_ref[...] + b_ref[...]
    o_ref[...] = ALPHA * h + jnp.dot(y_ref[...], w_ref[...], preferred_element_type=F32)
    _cast_blocks(rest[:len(rest) // 2], rest[len(rest) // 2 + 1:])


def _out_proj(y, w, res, name, casts=(), tm_target=1056, tn_target=512):
    M, K = y.shape
    N = w.shape[1]
    tm = _pick_tile(M, tm_target, V7X_BF16_ROWS)
    tn = _pick_tile(N, tn_target, V7X_LANES)
    grid = (M // tm, N // tn)
    c_in, c_out, c_shape, c_bytes = _cast_specs(casts, grid)
    blk = _nbytes((tm, K), BF16) + _nbytes((K, tn), BF16) + 3 * _nbytes((tm, tn), F32) + c_bytes
    tile = pl.BlockSpec((tm, tn), lambda i, j: (i, j))
    in_specs = [pl.BlockSpec((tm, K), lambda i, j: (i, 0)), pl.BlockSpec((K, tn), lambda i, j: (0, j)), tile]
    if len(res) == 1:
        body, args = _mm_res_body, (y, w, res[0])
    else:
        z, mu, rs, gamma, beta = res
        stat = pl.BlockSpec((tm, V7X_LANES), lambda i, j: (i, 0))
        vec = pl.BlockSpec((1, tn), lambda i, j: (0, j))
        in_specs += [stat, stat, vec, vec]
        body, args = _mm_lnres_body, (y, w, z, mu, rs, gamma.reshape(1, N), beta.reshape(1, N))
    outs = pl.pallas_call(
        body,
        grid=grid,
        in_specs=in_specs + c_in,
        out_specs=[tile] + c_out,
        out_shape=[jax.ShapeDtypeStruct((M, N), F32)] + c_shape,
        compiler_params=_params(("parallel", "arbitrary"), blk),
        name=name,
    )(*args, *casts)
    return outs[0], tuple(outs[1:])


LN_ROWS = 16
LN_UNROLL = 4


def _row_sum(x):
    w = x.shape[1]
    while w > V7X_LANES and w % (2 * V7X_LANES) == 0:
        w //= 2
        x = x[:, :w] + x[:, w:]
    return jnp.sum(x, axis=1, keepdims=True)


def _ln_rows(r):
    return pl.ds(pl.multiple_of(r * LN_ROWS, LN_ROWS), LN_ROWS)


def _ln_stats(load, n, mu_ref, rs_ref, d_model):
    inv_d = 1.0 / d_model
    stat = (LN_ROWS, V7X_LANES)

    def mean_pass(r, carry):
        mu_ref[_ln_rows(r), :] = jnp.broadcast_to(_row_sum(load(r)) * inv_d, stat)
        return carry

    def var_pass(r, carry):
        d = load(r) - mu_ref[_ln_rows(r), 0:1]
        rs_ref[_ln_rows(r), :] = jnp.broadcast_to(lax.rsqrt(_row_sum(d * d) * inv_d + LN_EPS), stat)
        return carry

    lax.fori_loop(0, n, mean_pass, 0, unroll=LN_UNROLL)
    lax.fori_loop(0, n, var_pass, 0, unroll=LN_UNROLL)


def _ln_apply(zr, r, mu_ref, rs_ref, g_ref, b_ref):
    return (zr - mu_ref[_ln_rows(r), 0:1]) * rs_ref[_ln_rows(r), 0:1] * g_ref[...] + b_ref[...]


def _ln_body(z_ref, g_ref, b_ref, ohb_ref, mu_ref, rs_ref):
    tr, D = z_ref.shape
    n = tr // LN_ROWS

    def load(r):
        return z_ref[_ln_rows(r), :]

    _ln_stats(load, n, mu_ref, rs_ref, D)

    def norm_pass(r, carry):
        ohb_ref[_ln_rows(r), :] = _ln_apply(load(r), r, mu_ref, rs_ref, g_ref, b_ref).astype(BF16)
        return carry

    lax.fori_loop(0, n, norm_pass, 0, unroll=2)


def _layer_norm(z, gamma, beta, name, tr_target=256):
    M, D = z.shape
    tr = _pick_tile(M, tr_target, LN_ROWS * LN_UNROLL)
    blk = _nbytes((tr, D), F32) + _nbytes((tr, D), BF16)
    stat = pl.BlockSpec((tr, V7X_LANES), lambda i: (i, 0))
    return pl.pallas_call(
        _ln_body,
        grid=(M // tr,),
        in_specs=[pl.BlockSpec((tr, D), lambda i: (i, 0)),
                  pl.BlockSpec((1, D), lambda i: (0, 0)),
                  pl.BlockSpec((1, D), lambda i: (0, 0))],
        out_specs=[pl.BlockSpec((tr, D), lambda i: (i, 0)), stat, stat],
        out_shape=[jax.ShapeDtypeStruct((M, D), BF16),
                   jax.ShapeDtypeStruct((M, V7X_LANES), F32),
                   jax.ShapeDtypeStruct((M, V7X_LANES), F32)],
        compiler_params=_params(("parallel",), blk),
        name=name,
    )(z, gamma.reshape(1, D), beta.reshape(1, D))


def _ln_final_body(za_ref, zb_ref, g_ref, b_ref, o_ref, mu_ref, rs_ref, *, shift):
    tr, D = o_ref.shape[1], o_ref.shape[2]
    n = tr // LN_ROWS

    def load(r):
        return za_ref[0, _ln_rows(r), :]

    _ln_stats(load, n, mu_ref, rs_ref, D)

    def norm_pass(r, carry):
        o_ref[0, _ln_rows(r - shift), :] = _ln_apply(load(r), r, mu_ref, rs_ref, g_ref, b_ref)
        return carry

    lax.fori_loop(shift, n, norm_pass, 0)
    for c in range(shift):
        zr = zb_ref[0, c * LN_ROWS:(c + 1) * LN_ROWS, :]
        mu = _row_sum(zr) * (1.0 / D)
        d = zr - mu
        rs = lax.rsqrt(_row_sum(d * d) * (1.0 / D) + LN_EPS)
        o_ref[0, (n - shift + c) * LN_ROWS:(n - shift + c + 1) * LN_ROWS, :] = d * rs * g_ref[...] + b_ref[...]


def _layer_norm_final(z, gamma, beta, B, T_pad, n_meta, S, name, tr_target=256):
    D = z.shape[1]
    assert n_meta % LN_ROWS == 0
    tr = _pick_tile(S, tr_target, LN_ROWS * LN_UNROLL)
    assert tr % n_meta == 0
    per = tr // n_meta
    blk = 2 * _nbytes((tr, D), F32) + _nbytes((n_meta, D), F32)
    z3 = z.reshape(B, T_pad, D)
    return pl.pallas_call(
        functools.partial(_ln_final_body, shift=n_meta // LN_ROWS),
        grid=(B, S // tr),
        in_specs=[pl.BlockSpec((1, tr, D), lambda b, t: (b, t, 0)),
                  pl.BlockSpec((1, n_meta, D), lambda b, t: (b, (t + 1) * per, 0)),
                  pl.BlockSpec((1, D), lambda b, t: (0, 0)),
                  pl.BlockSpec((1, D), lambda b, t: (0, 0))],
        out_specs=pl.BlockSpec((1, tr, D), lambda b, t: (b, t, 0)),
        out_shape=jax.ShapeDtypeStruct((B, S, D), F32),
        scratch_shapes=[pltpu.VMEM((tr, V7X_LANES), F32), pltpu.VMEM((tr, V7X_LANES), F32)],
        compiler_params=_params(("parallel", "parallel"), blk),
        name=name,
    )(z3, z3, gamma.reshape(1, D), beta.reshape(1, D))


SEG = V7X_SUBLANES
GROUP = SEG * SEG


def _sub_shift(v, d, fill, sub):
    return jnp.where(sub >= d, pltpu.roll(v, d, 0), fill)


def _rglru_body(*refs, conv_w, hpb, blk, nl):
    u_refs = refs[:nl]
    g_ref, cw_ref, cb_ref, wg_ref, bg_ref, lam_ref, y_ref = refs[nl:nl + 7]
    ucb, sa, sx, hs, tail, hcar = refs[nl + 7:]
    t = pl.program_id(2)
    tt = g_ref.shape[0]
    ngroup = tt // GROUP
    sub = lax.broadcasted_iota(jnp.int32, (SEG, V7X_LANES), 0)
    tile = (SEG, V7X_LANES)

    @pl.when(t == 0)
    def _():
        tail[...] = jnp.zeros_like(tail)
        hcar[...] = jnp.zeros_like(hcar)

    for c in range(nl):
        lanes = slice(c * V7X_LANES, (c + 1) * V7X_LANES)
        wk = [jnp.broadcast_to(cw_ref[k:k + 1, lanes], tile) for k in range(conv_w)]
        cb = jnp.broadcast_to(cb_ref[:, lanes], tile)
        prev = [tail[c, m] for m in range(conv_w - 1)]
        for gi in range(ngroup):
            cur = [u_refs[c][pl.ds(gi * GROUP + j, SEG, stride=SEG), :] for j in range(SEG)]
            ext = [jnp.where(sub == 0, pltpu.roll(prev[m], 1, 0), pltpu.roll(cur[SEG - (conv_w - 1) + m], 1, 0))
                   for m in range(conv_w - 1)]
            seq = ext + cur
            for j in range(SEG):
                acc = cb
                for k in range(conv_w):
                    acc = acc + wk[k] * seq[j + k]
                ucb[gi * GROUP + j * SEG:gi * GROUP + (j + 1) * SEG, lanes] = acc
            prev = cur[SEG - (conv_w - 1):]
        for m in range(conv_w - 1):
            tail[c, m] = prev[m]

    lam = lam_ref[...]
    logsig = -(jnp.maximum(-lam, 0.0) + jnp.log1p(jnp.exp(-jnp.abs(lam))))
    for hh in range(hpb):
        cols = slice(hh * blk, (hh + 1) * blk)
        ucs = ucb[:, cols]
        gate = jnp.dot(ucs.astype(BF16), wg_ref[hh], preferred_element_type=F32) + bg_ref[hh]
        r = _sigmoid(gate[:, :blk])
        i = _sigmoid(gate[:, blk:])
        a = jnp.exp(LRU_C * r * logsig[:, cols])
        sa[:, cols] = a
        sx[:, cols] = (ucs * i) * jnp.sqrt(1.0 - a * a)

    for c in range(nl):
        lanes = slice(c * V7X_LANES, (c + 1) * V7X_LANES)
        h_in = hcar[c]
        for gi in range(ngroup):
            hloc, ploc = [], []
            for j in range(SEG):
                rows = slice(gi * GROUP + j * SEG, gi * GROUP + (j + 1) * SEG)
                a = sa[rows, lanes]
                x = sx[rows, lanes]
                hloc.append(x if j == 0 else a * hloc[-1] + x)
                ploc.append(a if j == 0 else a * ploc[-1])
            at, xt = ploc[-1], hloc[-1]
            d = 1
            while d < SEG:
                xt = at * _sub_shift(xt, d, 0.0, sub) + xt
                at = at * _sub_shift(at, d, 1.0, sub)
                d *= 2
            carry = jnp.where(sub == 0, h_in, pltpu.roll(xt, 1, 0) + pltpu.roll(at, 1, 0) * h_in)
            for j in range(SEG):
                h = hloc[j] + ploc[j] * carry
                hs[c, pl.ds(gi * GROUP + j, SEG, stride=SEG), :] = h
            h_in = jnp.broadcast_to(h[SEG - 1:SEG, :], tile)
        hcar[c] = h_in

    for c in range(nl):
        lanes = slice(c * V7X_LANES, (c + 1) * V7X_LANES)
        y_ref[:, lanes] = (hs[c] * _silu(g_ref[:, lanes])).astype(BF16)


def _rglru_mix(ug, conv_w, conv_b, w_a, b_a, w_x, b_x, lam, B, T_pad):
    BR = conv_b.shape[0]
    heads, blk, _ = w_a.shape
    K = conv_w.shape[0]
    hpb = 2 if heads % 2 == 0 else 1
    cw = hpb * blk
    nl = cw // V7X_LANES
    ncb = BR // cw
    tt = _pick_tile(T_pad, 384, GROUP)
    nT = T_pad // tt
    wg = jnp.concatenate([w_a, w_x], axis=-1).astype(BF16)
    bg = jnp.concatenate([b_a.reshape(heads, 1, blk), b_x.reshape(heads, 1, blk)], axis=-1)
    body = functools.partial(_rglru_body, conv_w=K, hpb=hpb, blk=blk, nl=nl)
    blkbytes = 7 * _nbytes((tt, cw), F32) + _nbytes((hpb, blk, 2 * blk), BF16)

    def u_spec(l):
        return pl.BlockSpec((tt, V7X_LANES), lambda b, c, t: (b * nT + t, c * nl + l))

    return pl.pallas_call(
        body,
        grid=(B, ncb, nT),
        in_specs=[u_spec(l) for l in range(nl)] + [
            pl.BlockSpec((tt, cw), lambda b, c, t: (b * nT + t, c + ncb)),
            pl.BlockSpec((K, cw), lambda b, c, t: (0, c)),
            pl.BlockSpec((1, cw), lambda b, c, t: (0, c)),
            pl.BlockSpec((hpb, blk, 2 * blk), lambda b, c, t: (c, 0, 0)),
            pl.BlockSpec((hpb, 1, 2 * blk), lambda b, c, t: (c, 0, 0)),
            pl.BlockSpec((1, cw), lambda b, c, t: (0, c))],
        out_specs=pl.BlockSpec((tt, cw), lambda b, c, t: (b * nT + t, c)),
        out_shape=jax.ShapeDtypeStruct((B * T_pad, BR), BF16),
        scratch_shapes=[pltpu.VMEM((tt, cw), F32),
                        pltpu.VMEM((tt, cw), F32),
                        pltpu.VMEM((tt, cw), F32),
                        pltpu.VMEM((nl, tt, V7X_LANES), F32),
                        pltpu.VMEM((nl, K - 1, SEG, V7X_LANES), F32),
                        pltpu.VMEM((nl, SEG, V7X_LANES), F32)],
        compiler_params=_params(("parallel", "parallel", "arbitrary"), blkbytes),
        name="rglru_mix",
    )(*([ug] * nl), ug, conv_w.reshape(K, BR), conv_b.reshape(1, BR), wg, bg, lam.reshape(1, BR))


POOL_HIST = 24


def _pool_body(u_ref, g_ref, w_ref, sc_ref, y_ref, ubuf, sbuf, pbuf, *, windows):
    grp = pl.program_id(0)
    t = pl.program_id(2)
    tt = u_ref.shape[1]
    gw = u_ref.shape[2]
    n = POOL_HIST + tt

    @pl.when(t == 0)
    def _():
        ubuf[0:POOL_HIST, :] = jnp.zeros((POOL_HIST, gw), F32)

    sbuf[0:V7X_SUBLANES, :] = jnp.zeros((V7X_SUBLANES, gw), F32)
    ubuf[POOL_HIST:n, :] = u_ref[0]
    pos = lax.broadcasted_iota(jnp.int32, (tt, 1), 0) + t * tt + 1

    for gi, w in enumerate(windows):
        @pl.when(grp == gi)
        def _(w=w):
            lo = V7X_SUBLANES
            s = ubuf[lo:n, :] + ubuf[lo - 1:n - 1, :]
            d = 2
            while d < w:
                sbuf[lo:n, :] = s
                s = sbuf[lo:n, :] + sbuf[lo - d:n - d, :]
                d *= 2
            wsum = s[POOL_HIST - lo:, :]
            cnt = jnp.minimum(pos, w).astype(F32)
            pbuf[...] = (wsum / cnt - u_ref[0]).astype(BF16)

    ubuf[0:POOL_HIST, :] = ubuf[tt:n, :]
    mixed = jnp.dot(pbuf[...], w_ref[0], preferred_element_type=F32) * sc_ref[...]
    y_ref[0] = (mixed * _silu(g_ref[0])).astype(BF16)


def _pool_mix(ug, w_grp, scale, B, T_pad):
    ngrp, gw, _ = w_grp.shape
    BR = ngrp * gw
    tt = _pick_tile(T_pad, 1056, V7X_BF16_ROWS)
    ug3 = ug.reshape(B, T_pad, 2 * BR)
    body = functools.partial(_pool_body, windows=POOL_WINDOWS[:ngrp])
    blkbytes = 5 * _nbytes((tt, gw), F32) + _nbytes((gw, gw), BF16)
    return pl.pallas_call(
        body,
        grid=(ngrp, B, T_pad // tt),
        in_specs=[pl.BlockSpec((1, tt, gw), lambda g, b, t: (b, t, g)),
                  pl.BlockSpec((1, tt, gw), lambda g, b, t: (b, t, g + ngrp)),
                  pl.BlockSpec((1, gw, gw), lambda g, b, t: (g, 0, 0)),
                  pl.BlockSpec((1, gw), lambda g, b, t: (0, g))],
        out_specs=pl.BlockSpec((1, tt, gw), lambda g, b, t: (b, t, g)),
        out_shape=jax.ShapeDtypeStruct((B, T_pad, BR), BF16),
        scratch_shapes=[pltpu.VMEM((POOL_HIST + tt, gw), F32),
                        pltpu.VMEM((POOL_HIST + tt, gw), F32),
                        pltpu.VMEM((tt, gw), BF16)],
        compiler_params=_params(("parallel", "parallel", "arbitrary"), blkbytes),
        name="pool_mix",
    )(ug3, ug3, w_grp.astype(BF16), scale.reshape(1, BR))


def _rope_pairs(x, c_ref, s_ref):
    half = MLA_ROPE // 2
    lane = lax.broadcasted_iota(jnp.int32, x.shape, 1)
    partner = jnp.where(lane % MLA_ROPE < half,
                        pltpu.roll(x, V7X_LANES - half, 1), pltpu.roll(x, half, 1))
    return x * c_ref[...] + partner * s_ref[...]


def _rms_scale(x, g):
    return x * lax.rsqrt(jnp.mean(jnp.square(x), axis=-1, keepdims=True) + RMS_EPS) * g


def _qproj_body(c_ref, n_ref, w_ref, cos_ref, sin_ref, q_ref, xn_ref, *, ppb):
    @pl.when(pl.program_id(1) == 0)
    def _():
        xn_ref[...] = _rms_scale(c_ref[...], n_ref[...]).astype(BF16)

    res = jnp.dot(xn_ref[...], w_ref[...], preferred_element_type=F32)
    nn = 2 * MLA_NOPE
    pw = nn + 2 * MLA_ROPE
    for p in range(ppb):
        q_ref[:, p * pw:p * pw + nn] = res[:, p * pw:p * pw + nn].astype(BF16)
        q_ref[:, p * pw + nn:(p + 1) * pw] = _rope_pairs(
            res[:, p * pw + nn:(p + 1) * pw], cos_ref, sin_ref).astype(BF16)


def _kvproj_body(c_ref, kr_ref, n_ref, w_ref, cos_ref, sin_ref, kv_ref, krd_ref, xn_ref):
    @pl.when(pl.program_id(1) == 0)
    def _():
        xn_ref[...] = _rms_scale(c_ref[...], n_ref[...]).astype(BF16)
        rot = _rope_pairs(kr_ref[...], cos_ref, sin_ref)
        lane = lax.broadcasted_iota(jnp.int32, rot.shape, 1)
        krd_ref[...] = jnp.where(lane < MLA_ROPE, rot, pltpu.roll(rot, MLA_ROPE, 1)).astype(BF16)

    kv_ref[...] = jnp.dot(xn_ref[...], w_ref[...], preferred_element_type=F32).astype(BF16)


def _attn_body(q_ref, kn_ref, v_ref, kr_ref, g_ref, y_ref, m_ref, acc_ref, *, scale2, pps):
    qi = pl.program_id(2)
    tq = q_ref.shape[1]
    wide = 2 * tq
    nn = 2 * MLA_NOPE
    pw = nn + 2 * MLA_ROPE
    heads = [(pp, hh) for pp in range(pps) for hh in range(2)]
    lane = lax.broadcasted_iota(jnp.int32, (tq, V7X_LANES), 1)
    qs = []
    for pp, hh in heads:
        qr = q_ref[0, :, pp * pw + nn:(pp + 1) * pw]
        mine = (lane < MLA_ROPE) if hh == 0 else (lane >= MLA_ROPE)
        nope = q_ref[0, :, pp * pw + hh * MLA_NOPE:pp * pw + (hh + 1) * MLA_NOPE]
        qs.append(jnp.concatenate([nope, jnp.where(mine, qr, jnp.zeros_like(qr))], axis=1))

    m_ref[...] = jnp.full_like(m_ref, -jnp.inf)
    acc_ref[...] = jnp.zeros_like(acc_ref)

    def block(k0, tk, mask_shift):
        krope = kr_ref[0, pl.ds(k0, tk), :]
        ones = jnp.ones((tk, V7X_LANES), BF16)
        nc = tk // V7X_LANES
        for h, (pp, hh) in enumerate(heads):
            hc = slice((2 * pp + hh) * MLA_NOPE, (2 * pp + hh + 1) * MLA_NOPE)
            kf = jnp.concatenate([kn_ref[0, pl.ds(k0, tk), hc], krope], axis=1)
            s = lax.dot_general(qs[h], kf, (((1,), (1,)), ((), ())),
                                preferred_element_type=F32) * scale2
            if mask_shift is not None:
                rq = lax.broadcasted_iota(jnp.int32, (tq, tk), 0)
                ck = lax.broadcasted_iota(jnp.int32, (tq, tk), 1)
                s = jnp.where(ck <= rq + mask_shift, s, -1e30)
            cols = [s[:, c * V7X_LANES:(c + 1) * V7X_LANES] for c in range(nc)]
            mloc = cols[0]
            for c in range(1, nc):
                mloc = jnp.maximum(mloc, cols[c])
            m_old = m_ref[h]
            m_new = jnp.maximum(m_old, jnp.max(mloc, axis=1, keepdims=True))
            p = jnp.concatenate([jnp.exp2(cols[c] - m_new) for c in range(nc)], axis=1).astype(BF16)
            corr = jnp.exp2(m_old - m_new)
            vf = jnp.concatenate([v_ref[0, pl.ds(k0, tk), hc], ones], axis=1)
            pv = jnp.dot(p, vf, preferred_element_type=F32)
            acc_ref[h] = jnp.concatenate([corr, corr], axis=1) * acc_ref[h] + pv
            m_ref[h] = m_new

    nblk = qi // 2

    def loop_body(j, carry):
        block(pl.multiple_of(2 * j * wide, wide), wide, None)
        block(pl.multiple_of((2 * j + 1) * wide, wide), wide, None)
        return carry

    lax.fori_loop(0, nblk // 2, loop_body, 0)

    @pl.when(nblk % 2 == 1)
    def _():
        block(pl.multiple_of((nblk - 1) * wide, wide), wide, None)

    @pl.when(qi % 2 == 1)
    def _():
        block(pl.multiple_of((qi - 1) * tq, tq), wide, tq)

    @pl.when(qi % 2 == 0)
    def _():
        block(pl.multiple_of(qi * tq, tq), tq, 0)

    gate = _silu(g_ref[0])
    for h, (pp, hh) in enumerate(heads):
        hc = slice((2 * pp + hh) * MLA_NOPE, (2 * pp + hh + 1) * MLA_NOPE)
        acc = acc_ref[h]
        o = acc[:, :MLA_NOPE] / acc[:, MLA_NOPE:]
        y_ref[0, :, hc] = (o * gate[:, hc]).astype(BF16)


def _mla_mix(hb, w_in_p, q_norm, w_uq, kv_norm, w_ukv, B, T_pad):
    q_lora = q_norm.shape[0]
    kv_lora = kv_norm.shape[0]
    heads = w_uq.shape[1] // (MLA_NOPE + MLA_ROPE)
    vdim = w_ukv.shape[1] // heads - MLA_NOPE
    assert vdim == MLA_NOPE and heads % 2 == 0
    BR = heads * vdim
    npair = heads // 2
    M = hb.shape[0]
    D = hb.shape[1]

    c_all = _matmul(hb, w_in_p, F32, "mla_in", tn_target=1152)
    n_all = w_in_p.shape[1]

    tm = _pick_tile(T_pad, 1056, V7X_BF16_ROWS)
    nt = T_pad // tm

    half = MLA_ROPE // 2
    inv = ROPE_BASE ** (-jnp.arange(0, MLA_ROPE, 2, dtype=F32) / MLA_ROPE)
    ang = jnp.arange(T_pad, dtype=F32)[:, None] * inv[None, :]
    cos_t = jnp.tile(jnp.cos(ang), (1, 4))
    sin_t = jnp.tile(jnp.concatenate([-jnp.sin(ang), jnp.sin(ang)], axis=1), (1, 2))
    del half

    hd = MLA_NOPE + MLA_ROPE
    wq = w_uq.reshape(q_lora, npair, 2, hd)
    wq = jnp.concatenate([wq[..., 0, :MLA_NOPE], wq[..., 1, :MLA_NOPE],
                          wq[..., 0, MLA_NOPE:], wq[..., 1, MLA_NOPE:]], axis=-1)
    pw = 2 * hd
    wq = wq.reshape(q_lora, npair * pw).astype(BF16)
    cq_blk = BR // q_lora
    ppb = _pick_tile(npair, 4, 1)
    qw = ppb * pw
    q = pl.pallas_call(
        functools.partial(_qproj_body, ppb=ppb),
        grid=(M // tm, npair // ppb),
        in_specs=[pl.BlockSpec((tm, q_lora), lambda i, j: (i, cq_blk)),
                  pl.BlockSpec((1, q_lora), lambda i, j: (0, 0)),
                  pl.BlockSpec((q_lora, qw), lambda i, j: (0, j)),
                  pl.BlockSpec((tm, V7X_LANES), lambda i, j: (i % nt, 0)),
                  pl.BlockSpec((tm, V7X_LANES), lambda i, j: (i % nt, 0))],
        out_specs=pl.BlockSpec((tm, qw), lambda i, j: (i, j)),
        out_shape=jax.ShapeDtypeStruct((M, npair * pw), BF16),
        scratch_shapes=[pltpu.VMEM((tm, q_lora), BF16)],
        compiler_params=_params(("parallel", "arbitrary"),
                                _nbytes((tm, q_lora), F32) + _nbytes((q_lora, qw), BF16)
                                + 2 * _nbytes((tm, qw), F32)),
        name="mla_qproj",
    )(c_all, q_norm.reshape(1, q_lora), wq, cos_t, sin_t)

    wkv = w_ukv.reshape(kv_lora, heads, MLA_NOPE + vdim)
    wkv = jnp.concatenate([wkv[..., :MLA_NOPE].reshape(kv_lora, BR),
                           wkv[..., MLA_NOPE:].reshape(kv_lora, BR)], axis=1).astype(BF16)
    tn = _pick_tile(2 * BR, 1024, V7X_LANES)
    ckv_blk = (BR + q_lora) // kv_lora
    kr_blk = (BR + q_lora + kv_lora) // V7X_LANES
    kv, krd = pl.pallas_call(
        _kvproj_body,
        grid=(M // tm, 2 * BR // tn),
        in_specs=[pl.BlockSpec((tm, kv_lora), lambda i, j: (i, ckv_blk)),
                  pl.BlockSpec((tm, V7X_LANES), lambda i, j: (i, kr_blk)),
                  pl.BlockSpec((1, kv_lora), lambda i, j: (0, 0)),
                  pl.BlockSpec((kv_lora, tn), lambda i, j: (0, j)),
                  pl.BlockSpec((tm, V7X_LANES), lambda i, j: (i % nt, 0)),
                  pl.BlockSpec((tm, V7X_LANES), lambda i, j: (i % nt, 0))],
        out_specs=[pl.BlockSpec((tm, tn), lambda i, j: (i, j)),
                   pl.BlockSpec((tm, V7X_LANES), lambda i, j: (i, 0))],
        out_shape=[jax.ShapeDtypeStruct((M, 2 * BR), BF16),
                   jax.ShapeDtypeStruct((M, V7X_LANES), BF16)],
        scratch_shapes=[pltpu.VMEM((tm, kv_lora), BF16)],
        compiler_params=_params(("parallel", "arbitrary"),
                                _nbytes((tm, kv_lora), F32) + _nbytes((kv_lora, tn), BF16)
                                + 2 * _nbytes((tm, tn), F32)),
        name="mla_kvproj",
    )(c_all, c_all, kv_norm.reshape(1, kv_lora), wkv, cos_t, sin_t)

    tq = _pick_tile(T_pad, 384, V7X_BF16_ROWS)
    pps = _pick_tile(npair, 4, 1)
    ngrp = npair // pps
    nh = 2 * pps
    pv = nh * vdim
    qw = pps * pw
    q3 = q.reshape(B, T_pad, npair * pw)
    kv3 = kv.reshape(B, T_pad, 2 * BR)
    krd3 = krd.reshape(B, T_pad, V7X_LANES)
    c3 = c_all.reshape(B, T_pad, n_all)
    body = functools.partial(_attn_body, scale2=float(hd) ** -0.5 * LOG2_E, pps=pps)
    blkbytes = (_nbytes((tq, qw), BF16) + 2 * _nbytes((T_pad, pv), BF16) + _nbytes((T_pad, V7X_LANES), BF16)
                + _nbytes((tq, pv), F32) + _nbytes((tq, pv), BF16) + 2 * nh * _nbytes((tq, 2 * tq), F32))
    return pl.pallas_call(
        body,
        grid=(B, ngrp, T_pad // tq),
        in_specs=[pl.BlockSpec((1, tq, qw), lambda b, p, i: (b, i, p)),
                  pl.BlockSpec((1, T_pad, pv), lambda b, p, i: (b, 0, p)),
                  pl.BlockSpec((1, T_pad, pv), lambda b, p, i: (b, 0, p + ngrp)),
                  pl.BlockSpec((1, T_pad, V7X_LANES), lambda b, p, i: (b, 0, 0)),
                  pl.BlockSpec((1, tq, pv), lambda b, p, i: (b, i, p))],
        out_specs=pl.BlockSpec((1, tq, pv), lambda b, p, i: (b, i, p)),
        out_shape=jax.ShapeDtypeStruct((B, T_pad, BR), BF16),
        scratch_shapes=[pltpu.VMEM((nh, tq, V7X_LANES), F32),
                        pltpu.VMEM((nh, tq, 2 * vdim), F32)],
        compiler_params=_params(("parallel", "parallel", "arbitrary"), blkbytes),
        name="mla_attn",
    )(q3, kv3, kv3, krd3, c3).reshape(M, BR)


def _mm_rope_body(a_ref, w_ref, cos_ref, sin_ref, o_ref, *, dk, scale):
    res = jnp.dot(a_ref[...], w_ref[...], preferred_element_type=F32)
    cos = cos_ref[...]
    sin = sin_ref[...]
    hd = dk // 2
    for h in range(o_ref.shape[1] // dk):
        x1 = res[:, h * dk:h * dk + hd]
        x2 = res[:, h * dk + hd:(h + 1) * dk]
        o_ref[:, h * dk:h * dk + hd] = ((x1 * cos - x2 * sin) * scale).astype(o_ref.dtype)
        o_ref[:, h * dk + hd:(h + 1) * dk] = ((x2 * cos + x1 * sin) * scale).astype(o_ref.dtype)


def _matmul_rope(a, w, cols, cos_t, sin_t, dk, scale, out_dtype, name, tm_target=1056, tn_target=1024):
    M, K = a.shape
    start, N = cols
    T_pad = cos_t.shape[0]
    tm = _pick_tile(T_pad, tm_target, V7X_BF16_ROWS)
    tn = _pick_tile(N, tn_target, dk)
    assert start % tn == 0
    j0 = start // tn
    nt = T_pad // tm
    blk = (_nbytes((tm, K), a.dtype) + _nbytes((K, tn), w.dtype) + 2 * _nbytes((tm, tn), F32)
           + 2 * _nbytes((tm, dk // 2), F32))
    return pl.pallas_call(
        functools.partial(_mm_rope_body, dk=dk, scale=scale),
        grid=(M // tm, N // tn),
        in_specs=[pl.BlockSpec((tm, K), lambda i, j: (i, 0)),
                  pl.BlockSpec((K, tn), lambda i, j: (0, j + j0)),
                  pl.BlockSpec((tm, dk // 2), lambda i, j: (i % nt, 0)),
                  pl.BlockSpec((tm, dk // 2), lambda i, j: (i % nt, 0))],
        out_specs=pl.BlockSpec((tm, tn), lambda i, j: (i, j)),
        out_shape=jax.ShapeDtypeStruct((M, N), out_dtype),
        compiler_params=_params(("parallel", "arbitrary"), blk),
        name=name,
    )(a, w, cos_t, sin_t)


def _ret_body(lg_ref, q_ref, k_ref, v_ref, g_ref, y_ref, st_ref, dec_ref, *, hps, dk):
    c = pl.program_id(2)
    C = q_ref.shape[1]

    @pl.when(c == 0)
    def _():
        st_ref[...] = jnp.zeros_like(st_ref)
        ri = lax.broadcasted_iota(jnp.int32, (C, C), 0)
        ci = lax.broadcasted_iota(jnp.int32, (C, C), 1)
        diff = (ri - ci).astype(F32)
        for h in range(hps):
            dec_ref[h] = jnp.where(diff >= 0, jnp.exp(jnp.maximum(diff, 0.0) * lg_ref[h, :, 0:1]), 0.0)

    jcol = lax.broadcasted_iota(jnp.int32, (C, 1), 0).astype(F32)
    for h in range(hps):
        cols = slice(h * dk, (h + 1) * dk)
        lg = lg_ref[h, :, 0:1]
        qb = q_ref[0, :, cols]
        kf = k_ref[0, :, cols]
        vb = v_ref[0, :, cols]
        s = lax.dot_general(qb, kf.astype(BF16), (((1,), (1,)), ((), ())), preferred_element_type=F32)
        intra = jnp.dot((s * dec_ref[h]).astype(BF16), vb, preferred_element_type=F32)
        state = st_ref[h]
        inter = jnp.dot(qb, state.astype(BF16), preferred_element_type=F32) * jnp.exp((jcol + 1.0) * lg)
        kw = (kf * jnp.exp((C - 1.0 - jcol) * lg)).astype(BF16)
        st_ref[h] = jnp.exp(C * lg) * state + lax.dot_general(
            kw, vb, (((0,), (0,)), ((), ())), preferred_element_type=F32)
        o = intra + inter
        mu = jnp.mean(o, axis=-1, keepdims=True)
        var = jnp.mean(jnp.square(o - mu), axis=-1, keepdims=True)
        o = (o - mu) * lax.rsqrt(var + LN_EPS)
        y_ref[0, :, cols] = (o * _silu(g_ref[0, :, cols])).astype(BF16)


def _ret_mix(hb, wb, heads, B, T_pad):
    BR = wb.shape[1] // 4
    dk = BR // heads
    C = _pick_tile(T_pad, 384, V7X_BF16_ROWS)
    hps = 2 if heads % 2 == 0 else 1
    inv = ROPE_BASE ** (-jnp.arange(0, dk, 2, dtype=F32) / dk)
    ang = jnp.arange(T_pad, dtype=F32)[:, None] * inv[None, :]
    cos_t = jnp.cos(ang)
    sin_t = jnp.sin(ang)
    q =_matmul_rope(hb, wb, (0, BR), cos_t, sin_t, dk, 1.0, BF16, "l3_q")
    k = _matmul_rope(hb, wb, (BR, BR), cos_t, sin_t, dk, float(dk) ** -0.5, F32, "l3_k")
    v = _matmul(hb, wb, BF16, "l3_v", cols=(2 * BR, BR))
    g = _matmul(hb, wb, F32, "l3_g", cols=(3 * BR, BR))

    log_g = jnp.log(1.0 - 2.0 ** (-5.0 - jnp.arange(heads, dtype=F32)))
    lg = jnp.broadcast_to(log_g[:, None, None], (heads, 1, V7X_LANES))
    cw = hps * dk
    spec = pl.BlockSpec((1, C, cw), lambda b, h, c: (b, c, h))
    blkbytes = (2 * _nbytes((C, cw), F32) + 3 * _nbytes((C, cw), BF16)
                + hps * (4 * _nbytes((C, C), F32) + 2 * _nbytes((dk, dk), F32)))
    return pl.pallas_call(
        functools.partial(_ret_body, hps=hps, dk=dk),
        grid=(B, heads // hps, T_pad // C),
        in_specs=[pl.BlockSpec((hps, 1, V7X_LANES), lambda b, h, c: (h, 0, 0)), spec, spec, spec, spec],
        out_specs=spec,
        out_shape=jax.ShapeDtypeStruct((B, T_pad, BR), BF16),
        scratch_shapes=[pltpu.VMEM((hps, dk, dk), F32), pltpu.VMEM((hps, C, C), F32)],
        compiler_params=_params(("parallel", "parallel", "arbitrary"), blkbytes),
        name="ret_mix",
    )(lg, q.reshape(B, T_pad, BR), k.reshape(B, T_pad, BR), v.reshape(B, T_pad, BR),
      g.reshape(B, T_pad, BR)).reshape(B * T_pad, BR)


RET_HEADS = 16


def kernel(x, meta_tokens, l0_w_in, l0_conv_w, l0_conv_b, l0_w_a, l0_b_a, l0_w_x, l0_b_x, l0_lam, l0_w_out, l0_ln_g, l0_ln_b, l1_w_in, l1_w_grp, l1_scale, l1_w_out, l1_ln_g, l1_ln_b, l2_w_in, l2_q_norm, l2_w_uq, l2_kv_norm, l2_w_ukv, l2_w_out, l2_ln_g, l2_ln_b, l3_w_in, l3_w_out, l3_ln_g, l3_ln_b):
    B, S, D = x.shape
    n_meta = meta_tokens.shape[0]
    T = n_meta + S
    T_pad = -(-T // TIME_ALIGN) * TIME_ALIGN
    M = B * T_pad

    meta = jnp.broadcast_to(meta_tokens.astype(x.dtype)[None], (B, n_meta, D))
    h = jnp.concatenate([meta, x, jnp.zeros((B, T_pad - T, D), x.dtype)], axis=1).reshape(M, D)
    hb = h.astype(BF16)

    ug, (w0_out, w1_in) = _matmul(hb, l0_w_in.astype(BF16), F32, "l0_in", casts=(l0_w_out, l1_w_in))
    y = _rglru_mix(ug, l0_conv_w, l0_conv_b, l0_w_a, l0_b_a, l0_w_x, l0_b_x, l0_lam, B, T_pad)
    z, (w1_out,) = _out_proj(y.reshape(M, -1), w0_out, (h,), "l0_out_mm", casts=(l1_w_out,))
    hb, mu, rs = _layer_norm(z, l0_ln_g, l0_ln_b, "l0_out_ln")
    res = (z, mu, rs, l0_ln_g, l0_ln_b)

    ug, (w2_in, w2_out) = _matmul(hb, w1_in, F32, "l1_in", casts=(l2_w_in, l2_w_out))
    w2_in = jnp.pad(w2_in, ((0, 0), (0, -w2_in.shape[1] % V7X_LANES)))
    y = _pool_mix(ug, l1_w_grp, l1_scale, B, T_pad)
    z, (w3_in,) = _out_proj(y.reshape(M, -1), w1_out, res, "l1_out_mm", casts=(l3_w_in,))
    hb, mu, rs = _layer_norm(z, l1_ln_g, l1_ln_b, "l1_out_ln")
    res = (z, mu, rs, l1_ln_g, l1_ln_b)

    y = _mla_mix(hb, w2_in, l2_q_norm, l2_w_uq, l2_kv_norm, l2_w_ukv, B, T_pad)
    z, (w3_out,) = _out_proj(y, w2_out, res, "l2_out_mm", casts=(l3_w_out,))
    hb, mu, rs = _layer_norm(z, l2_ln_g, l2_ln_b, "l2_out_ln")
    res = (z, mu, rs, l2_ln_g, l2_ln_b)

    y = _ret_mix(hb, w3_in, RET_HEADS, B, T_pad)
    z, _ = _out_proj(y, w3_out, res, "l3_out_mm")
    return _layer_norm_final(z, l3_ln_g, l3_ln_b, B, T_pad, n_meta, S, "l3_out_ln")
```
